```python
import math
import jax, jax.numpy as jnp
from jax import lax
import numpy as np

D_MODEL = 2048
BATCH = 1
SEQ = 8192
DEPTH = 4

N_A_LAYERS = DEPTH // 2
N_B_LAYERS = DEPTH - N_A_LAYERS

N_HEADS_A = 16
HEAD_DIM_A = D_MODEL // N_HEADS_A
MOBA_BLOCK = 256
MOBA_TOPK = 3
MOBA_QCHUNK = 64

N_HEADS_B = 16
Q_LORA = 768
KV_LORA = 512
QK_NOPE = 128
QK_ROPE = 64
V_DIM = 128
ROPE_THETA = 10000.0
MLA_QCHUNK = 128

D_FF = 5632
EPS = 1e-6

kernel_name = "yoco_moba_mla_macaron_alibi"


def rmsnorm(x, g):
    xf = x.astype(jnp.float32)
    y = xf * lax.rsqrt(jnp.mean(xf * xf, axis=-1, keepdims=True) + EPS)
    return (y * g.astype(jnp.float32)).astype(x.dtype)


def swiglu(xn, w_gu, w_d):
    gate, up = jnp.split(xn @ w_gu, 2, axis=-1)
    return (jax.nn.silu(gate) * up) @ w_d


def alibi_slopes(n):
    return jnp.asarray(np.array([2.0 ** (-8.0 * (i + 1) / n) for i in range(n)], dtype=np.float32))


def rope_tables(s):
    inv = 1.0 / (ROPE_THETA ** (jnp.arange(0, QK_ROPE, 2, dtype=jnp.float32) / QK_ROPE))
    ang = jnp.arange(s, dtype=jnp.float32)[:, None] * inv[None, :]
    return jnp.cos(ang), jnp.sin(ang)


def apply_rope(x, cos, sin):
    xf = x.astype(jnp.float32)
    x1, x2 = jnp.split(xf, 2, axis=-1)
    out = jnp.concatenate([x1 * cos - x2 * sin, x1 * sin + x2 * cos], axis=-1)
    return out.astype(x.dtype)


def moba_attention(xn, w_qkv, w_o, slopes):
    B, S, _ = xn.shape
    H, dh, BS, QC = N_HEADS_A, HEAD_DIM_A, MOBA_BLOCK, MOBA_QCHUNK
    nb = -(-S // BS)
    topk = min(MOBA_TOPK, nb)
    q, k, v = jnp.split(xn @ w_qkv, 3, axis=-1)

    def heads(t):
        return t.reshape(B, S, H, dh).transpose(0, 2, 1, 3)

    q, k, v = heads(q), heads(k), heads(v)
    pad = nb * BS - S
    k_p = jnp.pad(k, ((0, 0), (0, 0), (0, pad), (0, 0)))
    v_p = jnp.pad(v, ((0, 0), (0, 0), (0, pad), (0, 0)))
    kb = k_p.reshape(B, H, nb, BS, dh)
    vb = v_p.reshape(B, H, nb, BS, dh)
    k_mean = jnp.mean(kb.astype(jnp.float32), axis=3)
    n_chunks = S // QC
    q_chunks = q.reshape(B, H, n_chunks, QC, dh).transpose(2, 0, 1, 3, 4)
    m = slopes.reshape(1, H, 1, 1)
    scale = dh ** -0.5
    gather_blocks = jax.vmap(jax.vmap(lambda tab, idx: tab[idx]))
    blk_ids = jnp.arange(nb)

    def chunk(args):
        ci, qc = args
        t0 = ci * QC
        cur = t0 // BS
        t = t0 + jnp.arange(QC)
        gate = jnp.einsum('bhqd,bhnd->bhqn', qc.astype(jnp.float32), k_mean)
        gate = jnp.where(blk_ids < cur, gate, -jnp.inf)
        _, sel = lax.top_k(gate, topk)
        sel_ok = sel < cur
        k_sel = gather_blocks(kb, sel)
        v_sel = gather_blocks(vb, sel)
        s_sel = jnp.einsum('bhqd,bhqnkd->bhqnk', qc, k_sel).astype(jnp.float32) * scale
        key_pos = sel[..., None] * BS + jnp.arange(BS)
        dist = (t[:, None, None] - key_pos).astype(jnp.float32)
        s_sel = jnp.where(sel_ok[..., None], s_sel - m[..., None] * dist, -jnp.inf)
        s_sel = s_sel.reshape(B, H, QC, topk * BS)
        k_own = lax.dynamic_slice_in_dim(k_p, cur * BS, BS, axis=2)
        v_own = lax.dynamic_slice_in_dim(v_p, cur * BS, BS, axis=2)
        own_pos = cur * BS + jnp.arange(BS)
        dist_own = (t[:, None] - own_pos[None, :]).astype(jnp.float32)
        s_own = jnp.einsum('bhqd,bhkd->bhqk', qc, k_own).astype(jnp.float32) * scale - m * dist_own
        s_own = jnp.where(dist_own >= 0, s_own, -jnp.inf)
        p = jax.nn.softmax(jnp.concatenate([s_sel, s_own], axis=-1), axis=-1).astype(v.dtype)
        p_sel = p[..., :topk * BS].reshape(B, H, QC, topk, BS)
        p_own = p[..., topk * BS:]
        return (jnp.einsum('bhqnk,bhqnkd->bhqd', p_sel, v_sel)
                + jnp.einsum('bhqk,bhkd->bhqd', p_own, v_own))

    o = lax.map(chunk, (jnp.arange(n_chunks), q_chunks))
    o = o.transpose(1, 0, 3, 2, 4).reshape(B, S, H * dh)
    return o @ w_o


def mla_shared_kv(h, kv_norm, w_dkv, ckv_norm, w_ukv, cos, sin):
    B, S, _ = h.shape
    H = N_HEADS_B
    hn = rmsnorm(h, kv_norm)
    c_kv, k_rope = jnp.split(hn @ w_dkv, [KV_LORA], axis=-1)
    c_kv = rmsnorm(c_kv, ckv_norm)
    k_rope = apply_rope(k_rope, cos, sin)
    kv = (c_kv @ w_ukv).reshape(B, S, H, QK_NOPE + V_DIM)
    k_nope, v = jnp.split(kv, [QK_NOPE], axis=-1)
    k = jnp.concatenate([k_nope, jnp.broadcast_to(k_rope[:, :, None, :], (B, S, H, QK_ROPE))], axis=-1)
    return k.transpose(0, 2, 1, 3), v.transpose(0, 2, 1, 3)


def mla_attention(xn, w_dq, cq_norm, w_uq, w_o, k, v, cos, sin):
    B, S, _ = xn.shape
    H, QC = N_HEADS_B, MLA_QCHUNK
    cq = rmsnorm(xn @ w_dq, cq_norm)
    q = (cq @ w_uq).reshape(B, S, H, QK_NOPE + QK_ROPE)
    q_nope, q_rope = jnp.split(q, [QK_NOPE], axis=-1)
    q_rope = apply_rope(q_rope, cos[:, None, :], sin[:, None, :])
    q = jnp.concatenate([q_nope, q_rope], axis=-1).transpose(0, 2, 1, 3)
    n_chunks = S // QC
    q_chunks = q.reshape(B, H, n_chunks, QC, QK_NOPE + QK_ROPE).transpose(2, 0, 1, 3, 4)
    scale = (QK_NOPE + QK_ROPE) ** -0.5
    key_pos = jnp.arange(S)

    def chunk(args):
        ci, qc = args
        t = ci * QC + jnp.arange(QC)
        s = jnp.einsum('bhqd,bhkd->bhqk', qc, k).astype(jnp.float32) * scale
        s = jnp.where(key_pos[None, :] <= t[:, None], s, -jnp.inf)
        p = jax.nn.softmax(s, axis=-1).astype(v.dtype)
        return jnp.einsum('bhqk,bhkd->bhqd', p, v)

    o = lax.map(chunk, (jnp.arange(n_chunks), q_chunks))
    o = o.transpose(1, 0, 3, 2, 4).reshape(B, S, H * V_DIM)
    return o @ w_o


def setup_inputs(seed: int = 0) -> dict:
    key = jax.random.key(seed)
    ks = jax.random.split(key, 24)
    f32 = jnp.float32

    def w(k, shape, fan_in):
        return jax.random.normal(k, shape, f32) * (fan_in ** -0.5)

    def gain(k, shape):
        return 1.0 + 0.02 * jax.random.normal(k, shape, f32)

    D, F = D_MODEL, D_FF
    return {
        "x": jax.random.normal(ks[0], (BATCH, SEQ, D), f32),
        "ln_ffn1": gain(ks[1], (DEPTH, D)),
        "ffn1_wgu": w(ks[2], (DEPTH, D, 2 * F), D),
        "ffn1_wd": w(ks[3], (DEPTH, F, D), F),
        "ln_mix": gain(ks[4], (DEPTH, D)),
        "ln_ffn2": gain(ks[5], (DEPTH, D)),
        "ffn2_wgu": w(ks[6], (DEPTH, D, 2 * F), D),
        "ffn2_wd": w(ks[7], (DEPTH, F, D), F),
        "moba_wqkv": w(ks[8], (N_A_LAYERS, D, 3 * N_HEADS_A * HEAD_DIM_A), D),
        "moba_wo": w(ks[9], (N_A_LAYERS, N_HEADS_A * HEAD_DIM_A, D), N_HEADS_A * HEAD_DIM_A),
        "kv_norm": gain(ks[10], (D,)),
        "mla_wdkv": w(ks[11], (D, KV_LORA + QK_ROPE), D),
        "ckv_norm": gain(ks[12], (KV_LORA,)),
        "mla_wukv": w(ks[13], (KV_LORA, N_HEADS_B * (QK_NOPE + V_DIM)), KV_LORA),
        "mla_wdq": w(ks[14], (N_B_LAYERS, D, Q_LORA), D),
        "cq_norm": gain(ks[15], (N_B_LAYERS, Q_LORA)),
        "mla_wuq": w(ks[16], (N_B_LAYERS, Q_LORA, N_HEADS_B * (QK_NOPE + QK_ROPE)), Q_LORA),
        "mla_wo": w(ks[17], (N_B_LAYERS, N_HEADS_B * V_DIM, D), N_HEADS_B * V_DIM),
        "final_norm": gain(ks[18], (D,)),
    }


def reference(x, ln_ffn1, ffn1_wgu, ffn1_wd, ln_mix, ln_ffn2, ffn2_wgu, ffn2_wd,
              moba_wqkv, moba_wo, kv_norm, mla_wdkv, ckv_norm, mla_wukv,
              mla_wdq, cq_norm, mla_wuq, mla_wo, final_norm):
    S = x.shape[1]
    slopes = alibi_slopes(N_HEADS_A)
    cos, sin = rope_tables(S)

    def macaron_layer(h, l, mixer):
        h = h + 0.5 * swiglu(rmsnorm(h, ln_ffn1[l]), ffn1_wgu[l], ffn1_wd[l])
        h = h + mixer(rmsnorm(h, ln_mix[l]))
        h = h + 0.5 * swiglu(rmsnorm(h, ln_ffn2[l]), ffn2_wgu[l], ffn2_wd[l])
        return h

    h = x
    for l in range(N_A_LAYERS):
        h = macaron_layer(h, l, lambda xn, l=l: moba_attention(xn, moba_wqkv[l], moba_wo[l], slopes))
    k_sh, v_sh = mla_shared_kv(h, kv_norm, mla_wdkv, ckv_norm, mla_wukv, cos, sin)
    for j in range(N_B_LAYERS):
        l = N_A_LAYERS + j
        h = macaron_layer(h, l, lambda xn, j=j: mla_attention(
            xn, mla_wdq[j], cq_norm[j], mla_wuq[j], mla_wo[j], k_sh, v_sh, cos, sin))
    return rmsnorm(h, final_norm)
```

```python
import functools

import numpy as np
import jax
import jax.numpy as jnp
from jax import lax
from jax.experimental import pallas as pl
from jax.experimental.pallas import tpu as pltpu

F32 = jnp.float32
BF16 = jnp.bfloat16

EPS = 1e-6
ROPE_THETA = 10000.0

N_HEADS = 16
MOBA_BLOCK = 256
MOBA_TOPK = 3
KV_LORA = 512
QK_NOPE = 128
QK_ROPE = 64
V_DIM = 128

LANES = 128
HEAD_PAD = 2 * LANES
MASK_BIAS = -1e9
VMEM_LIMIT = 56 * 1024 * 1024

_ARB1 = ("arbitrary",)
_ARB2 = ("arbitrary", "arbitrary")


def _params(sem):
    return pltpu.CompilerParams(dimension_semantics=sem, vmem_limit_bytes=VMEM_LIMIT)


def _rms(x, g):
    ms = jnp.mean(x * x, axis=-1, keepdims=True)
    return x * lax.rsqrt(ms + EPS) * g


def _dot(a, b):
    return jnp.dot(a, b, preferred_element_type=F32)


def _dot_nt(a, b):
    return lax.dot_general(a, b, (((1,), (1,)), ((), ())), preferred_element_type=F32)


def _ffn_kernel(h_ref, g_ref, wg_ref, wu_ref, wd_ref, o_ref, xn_ref, acc_ref, *, nf):
    f = pl.program_id(1)

    @pl.when(f == 0)
    def _():
        xn_ref[...] = _rms(h_ref[...], g_ref[...]).astype(BF16)
        acc_ref[...] = jnp.zeros_like(acc_ref)

    xn = xn_ref[...]
    gate = _dot(xn, wg_ref[...])
    up = _dot(xn, wu_ref[...])
    act = (gate * jax.nn.sigmoid(gate) * up).astype(BF16)
    acc_ref[...] += _dot(act, wd_ref[...])

    @pl.when(f == nf - 1)
    def _():
        o_ref[...] = h_ref[...] + 0.5 * acc_ref[...]


def _ffn(h, ln, wgu, wd, layer, *, tm=512, tf=512):
    S, D = h.shape
    F = wd.shape[1]
    nf = F // tf
    return pl.pallas_call(
        functools.partial(_ffn_kernel, nf=nf),
        grid=(S // tm, nf),
        in_specs=[
            pl.BlockSpec((tm, D), lambda i, f: (i, 0)),
            pl.BlockSpec((None, 1, D), lambda i, f: (layer, 0, 0)),
            pl.BlockSpec((None, D, tf), lambda i, f: (layer, 0, f)),
            pl.BlockSpec((None, D, tf), lambda i, f: (layer, 0, f + nf)),
            pl.BlockSpec((None, tf, D), lambda i, f: (layer, f, 0)),
        ],
        out_specs=pl.BlockSpec((tm, D), lambda i, f: (i, 0)),
        out_shape=jax.ShapeDtypeStruct((S, D), F32),
        scratch_shapes=[pltpu.VMEM((tm, D), BF16), pltpu.VMEM((tm, D), F32)],
        compiler_params=_params(_ARB2),
        name="ffn",
    )(h, ln, wgu, wgu, wd)


def _qkv_kernel(h_ref, g_ref, w_ref, o_ref, xn_ref, *, n_q_blocks, scale):
    j = pl.program_id(1)

    @pl.when(j == 0)
    def _():
        xn_ref[...] = _rms(h_ref[...], g_ref[...]).astype(BF16)

    acc = _dot(xn_ref[...], w_ref[...])
    o_ref[...] = (acc * jnp.where(j < n_q_blocks, scale, 1.0)).astype(BF16)


def _moba_qkv(h, ln, w, layer, *, tm=512, tn=512):
    S, D = h.shape
    N = w.shape[2]
    dh = D // N_HEADS
    return pl.pallas_call(
        functools.partial(_qkv_kernel, n_q_blocks=D // tn, scale=float(dh) ** -0.5),
        grid=(S // tm, N // tn),
        in_specs=[
            pl.BlockSpec((tm, D), lambda i, j: (i, 0)),
            pl.BlockSpec((None, 1, D), lambda i, j: (layer, 0, 0)),
            pl.BlockSpec((None, D, tn), lambda i, j: (layer, 0, j)),
        ],
        out_specs=pl.BlockSpec((tm, tn), lambda i, j: (i, j)),
        out_shape=jax.ShapeDtypeStruct((S, N), BF16),
        scratch_shapes=[pltpu.VMEM((tm, D), BF16)],
        compiler_params=_params(_ARB2),
        name="moba_qkv",
    )(h, ln, w)


def _moba_kernel(slopes_ref, q_ref, k_ref, v_ref, o_ref, kaug_ref, kmean_ref, bias_ref,
                 *, nb, bs, topk):
    h = pl.program_id(0)
    qi = pl.program_id(1)
    slope = slopes_ref[h]
    dh = q_ref.shape[1]

    @pl.when(qi == 0)
    def _():
        kaug_ref[:, :dh] = k_ref[...]
        lane_b = lax.broadcasted_iota(jnp.int32, (bs, LANES), 1)
        kmean_ref[...] = jnp.zeros_like(kmean_ref)
        for n in range(nb):
            kaug_ref[n * bs:(n + 1) * bs, dh:] = (lane_b == n).astype(BF16)
            kmean_ref[n:n + 1, :] = jnp.mean(
                k_ref[n * bs:(n + 1) * bs, :].astype(F32), axis=0, keepdims=True)
        qpos = lax.broadcasted_iota(jnp.int32, (bs, bs), 0)
        kpos = lax.broadcasted_iota(jnp.int32, (bs, bs), 1)
        bias_ref[...] = -slope * (qpos - kpos).astype(F32)

    q = q_ref[...]

    gate = _dot_nt(q, kmean_ref[...].astype(BF16))
    lane = lax.broadcasted_iota(jnp.int32, (bs, LANES), 1)
    valid = lane < qi
    g = jnp.where(valid, gate, -jnp.inf)
    picked = jnp.zeros((bs, LANES), jnp.bool_)
    for _ in range(topk):
        mx = jnp.max(g, axis=-1, keepdims=True)
        idx = jnp.min(jnp.where(g == mx, lane, LANES), axis=-1, keepdims=True)
        hit = lane == idx
        picked = jnp.logical_or(picked, hit)
        g = jnp.where(hit, -jnp.inf, g)
    selbias = jnp.where(jnp.logical_and(picked, valid), 0.0, MASK_BIAS)
    q_aug = jnp.concatenate([q, selbias.astype(BF16)], axis=1)

    own = pl.multiple_of(qi * bs, bs)
    bias = bias_ref[...]
    s = _dot_nt(q, k_ref[pl.ds(own, bs), :]) + bias
    qpos = lax.broadcasted_iota(jnp.int32, (bs, bs), 0)
    kpos = lax.broadcasted_iota(jnp.int32, (bs, bs), 1)
    s = jnp.where(qpos >= kpos, s, -jnp.inf)
    m0 = jnp.max(s, axis=-1, keepdims=True)
    p = jnp.exp(s - m0)
    l0 = jnp.sum(p, axis=-1, keepdims=True)
    acc0 = _dot(p.astype(BF16), v_ref[pl.ds(own, bs), :])

    def body(n, carry):
        m, l, acc = carry
        start = pl.multiple_of(n * bs, bs)
        st = _dot_nt(q_aug, kaug_ref[pl.ds(start, bs), :]) + bias_ref[...]
        far = slope * ((qi - n) * bs).astype(F32)
        m_new = jnp.maximum(m, jnp.max(st, axis=-1, keepdims=True) - far)
        p = jnp.exp(st - (m_new + far))
        alpha = jnp.exp(m - m_new)
        l = alpha * l + jnp.sum(p, axis=-1, keepdims=True)
        acc = alpha * acc + _dot(p.astype(BF16), v_ref[pl.ds(start, bs), :])
        return m_new, l, acc

    _, l, acc = lax.fori_loop(0, qi, body, (m0, l0, acc0))
    o_ref[...] = (acc / l).astype(BF16)


def _moba_attention(qkv, slopes):
    S, N = qkv.shape
    H = N_HEADS
    dh = N // (3 * H)
    bs = MOBA_BLOCK
    nb = S // bs
    assert S % bs == 0 and nb <= LANES and dh == LANES
    return pl.pallas_call(
        functools.partial(_moba_kernel, nb=nb, bs=bs, topk=min(MOBA_TOPK, nb)),
        grid=(H, nb),
        in_specs=[
            pl.BlockSpec(memory_space=pltpu.SMEM),
            pl.BlockSpec((bs, dh), lambda h, i: (i, h)),
            pl.BlockSpec((S, dh), lambda h, i: (0, H + h)),
            pl.BlockSpec((S, dh), lambda h, i: (0, 2 * H + h)),
        ],
        out_specs=pl.BlockSpec((bs, dh), lambda h, i: (i, h)),
        out_shape=jax.ShapeDtypeStruct((S, H * dh), BF16),
        scratch_shapes=[
            pltpu.VMEM((S, dh + LANES), BF16),
            pltpu.VMEM((LANES, dh), F32),
            pltpu.VMEM((bs, bs), F32),
        ],
        compiler_params=_params(_ARB2),
        name="moba_attn",
    )(slopes, qkv, qkv, qkv)


def _oproj_kernel(h_ref, o_ref, w_ref, out_ref):
    out_ref[...] = h_ref[...] + _dot(o_ref[...], w_ref[...])


def _oproj(h, o, w, layer, *, tm=512):
    S, D = h.shape
    K = o.shape[1]
    return pl.pallas_call(
        _oproj_kernel,
        grid=(S // tm,),
        in_specs=[
            pl.BlockSpec((tm, D), lambda i: (i, 0)),
            pl.BlockSpec((tm, K), lambda i: (i, 0)),
            pl.BlockSpec((None, K, D), lambda i: (layer, 0, 0)),
        ],
        out_specs=pl.BlockSpec((tm, D), lambda i: (i, 0)),
        out_shape=jax.ShapeDtypeStruct((S, D), F32),
        compiler_params=_params(_ARB1),
        name="oproj",
    )(h, o, w)


def _rope_pair(pair, cs):
    y = pair * cs
    z = y + pltpu.roll(y, LANES // 2, 1)
    lane = lax.broadcasted_iota(jnp.int32, y.shape, 1)
    return jnp.where(lane < LANES // 2, z, 0.0)


def _mla_kv_kernel(h_ref, g_ref, wd_ref, cg_ref, wu_ref, cs_ref, k_ref, v_ref, *, nh):
    hn = _rms(h_ref[...], g_ref[...]).astype(BF16)
    t = _dot(hn, wd_ref[...])
    c = _rms(t[:, :KV_LORA], cg_ref[...]).astype(BF16)
    k_rope = _rope_pair(t[:, KV_LORA:], cs_ref[...])
    for hd in range(nh):
        a = _dot(c, wu_ref[:, hd * (QK_NOPE + V_DIM):(hd + 1) * (QK_NOPE + V_DIM)])
        k_ref[hd] = jnp.concatenate([a[:, :QK_NOPE], k_rope], axis=1).astype(BF16)
        v_ref[hd] = a[:, QK_NOPE:].astype(BF16)


def _mla_kv(h, g, wd, cg, wu, cs, *, tm=256):
    S, D = h.shape
    H = N_HEADS
    return pl.pallas_call(
        functools.partial(_mla_kv_kernel, nh=H),
        grid=(S // tm,),
        in_specs=[
            pl.BlockSpec((tm, D), lambda i: (i, 0)),
            pl.BlockSpec((1, D), lambda i: (0, 0)),
            pl.BlockSpec(wd.shape, lambda i: (0, 0)),
            pl.BlockSpec((1, KV_LORA), lambda i: (0, 0)),
            pl.BlockSpec(wu.shape, lambda i: (0, 0)),
            pl.BlockSpec((tm, LANES), lambda i: (i, 0)),
        ],
        out_specs=[
            pl.BlockSpec((H, tm, HEAD_PAD), lambda i: (0, i, 0)),
            pl.BlockSpec((H, tm, V_DIM), lambda i: (0, i, 0)),
        ],
        out_shape=[
            jax.ShapeDtypeStruct((H, S, HEAD_PAD), BF16),
            jax.ShapeDtypeStruct((H, S, V_DIM), BF16),
        ],
        compiler_params=_params(_ARB1),
        name="mla_kv",
    )(h, g, wd, cg, wu, cs)


def _mla_q_kernel(h_ref, g_ref, wd_ref, cg_ref, wu_ref, cs_ref, q_ref, *, nh, scale):
    xn = _rms(h_ref[...], g_ref[...]).astype(BF16)
    cq = _rms(_dot(xn, wd_ref[...]), cg_ref[...]).astype(BF16)
    cs = cs_ref[...]
    for hd in range(nh):
        a = _dot(cq, wu_ref[:, hd * HEAD_PAD:(hd + 1) * HEAD_PAD])
        q_rope = _rope_pair(a[:, QK_NOPE:], cs)
        q_ref[hd] = (jnp.concatenate([a[:, :QK_NOPE], q_rope], axis=1) * scale).astype(BF16)


def _mla_q(h, ln, wd, cg, wu, cs, layer, j, *, tm=256):
    S, D = h.shape
    H = N_HEADS
    R = wd.shape[2]
    return pl.pallas_call(
        functools.partial(_mla_q_kernel, nh=H, scale=float(QK_NOPE + QK_ROPE) ** -0.5),
        grid=(S // tm,),
        in_specs=[
            pl.BlockSpec((tm, D), lambda i: (i, 0)),
            pl.BlockSpec((None, 1, D), lambda i: (layer, 0, 0)),
            pl.BlockSpec((None, D, R), lambda i: (j, 0, 0)),
            pl.BlockSpec((None, 1, R), lambda i: (j, 0, 0)),
            pl.BlockSpec((None, R, H * HEAD_PAD), lambda i: (j, 0, 0)),
            pl.BlockSpec((tm, LANES), lambda i: (i, 0)),
        ],
        out_specs=pl.BlockSpec((H, tm, HEAD_PAD), lambda i: (0, i, 0)),
        out_shape=jax.ShapeDtypeStruct((H, S, HEAD_PAD), BF16),
        compiler_params=_params(_ARB1),
        name="mla_q",
    )(h, ln, wd, cg, wu, cs)


def _mla_attn_kernel(q_ref, k_ref, v_ref, o_ref, *, tq):
    qi = pl.program_id(1)
    q = q_ref[...]

    own = pl.multiple_of(qi * tq, tq)
    s = _dot_nt(q, k_ref[pl.ds(own, tq), :])
    qpos = lax.broadcasted_iota(jnp.int32, (tq, tq), 0)
    kpos = lax.broadcasted_iota(jnp.int32, (tq, tq), 1)
    s = jnp.where(qpos >= kpos, s, -jnp.inf)
    m0 = jnp.max(s, axis=-1, keepdims=True)
    p = jnp.exp(s - m0)
    l0 = jnp.sum(p, axis=-1, keepdims=True)
    acc0 = _dot(p.astype(BF16), v_ref[pl.ds(own, tq), :])

    def body(n, carry):
        m, l, acc = carry
        start = pl.multiple_of(n * tq, tq)
        s = _dot_nt(q, k_ref[pl.ds(start, tq), :])
        m_new = jnp.maximum(m, jnp.max(s, axis=-1, keepdims=True))
        p = jnp.exp(s - m_new)
        alpha = jnp.exp(m - m_new)
        l = alpha * l + jnp.sum(p, axis=-1, keepdims=True)
        acc = alpha * acc + _dot(p.astype(BF16), v_ref[pl.ds(start, tq), :])
        return m_new, l, acc

    _, l, acc = lax.fori_loop(0, qi, body, (m0, l0, acc0))
    o_ref[...] = (acc / l).astype(BF16)


def _mla_attention(q, k, v, *, tq=256):
    H, S, _ = q.shape
    return pl.pallas_call(
        functools.partial(_mla_attn_kernel, tq=tq),
        grid=(H, S // tq),
        in_specs=[
            pl.BlockSpec((None, tq, HEAD_PAD), lambda h, i: (h, i, 0)),
            pl.BlockSpec((None, S, HEAD_PAD), lambda h, i: (h, 0, 0)),
            pl.BlockSpec((None, S, V_DIM), lambda h, i: (h, 0, 0)),
        ],
        out_specs=pl.BlockSpec((tq, V_DIM), lambda h, i: (i, h)),
        out_shape=jax.ShapeDtypeStruct((S, H * V_DIM), BF16),
        compiler_params=_params(_ARB2),
        name="mla_attn",
    )(q, k, v)


def _norm_kernel(h_ref, g_ref, o_ref):
    o_ref[...] = _rms(h_ref[...], g_ref[...])


def _final_norm(h, g, *, tm=512):
    S, D = h.shape
    return pl.pallas_call(
        _norm_kernel,
        grid=(S // tm,),
        in_specs=[pl.BlockSpec((tm, D), lambda i: (i, 0)), pl.BlockSpec((1, D), lambda i: (0, 0))],
        out_specs=pl.BlockSpec((tm, D), lambda i: (i, 0)),
        out_shape=jax.ShapeDtypeStruct((S, D), F32),
        compiler_params=_params(_ARB1),
        name="final_norm",
    )(h, g)


def _rotate_half_cols(w):
    w1, w2 = jnp.split(w, 2, axis=-1)
    return jnp.concatenate([-w2, w1], axis=-1)


def _alibi_slopes(n):
    return jnp.asarray(np.array([2.0 ** (-8.0 * (i + 1) / n) for i in range(n)], dtype=np.float32))


def _rope_table(s):
    inv = 1.0 / (ROPE_THETA ** (jnp.arange(0, QK_ROPE, 2, dtype=F32) / QK_ROPE))
    ang = jnp.arange(s, dtype=F32)[:, None] * inv[None, :]
    cos, sin = jnp.cos(ang), jnp.sin(ang)
    return jnp.concatenate([cos, cos, sin, sin], axis=-1)


def kernel(x, ln_ffn1, ffn1_wgu, ffn1_wd, ln_mix, ln_ffn2, ffn2_wgu, ffn2_wd, moba_wqkv, moba_wo, kv_norm, mla_wdkv, ckv_norm, mla_wukv, mla_wdq, cq_norm, mla_wuq, mla_wo, final_norm):
    B, S, D = x.shape
    assert B == 1
    depth = ln_ffn1.shape[0]
    n_a = moba_wqkv.shape[0]
    n_b = mla_wdq.shape[0]
    H = N_HEADS

    ln1 = ln_ffn1.reshape(depth, 1, D)
    lnm = ln_mix.reshape(depth, 1, D)
    ln2 = ln_ffn2.reshape(depth, 1, D)
    wgu1, wd1 = ffn1_wgu.astype(BF16), ffn1_wd.astype(BF16)
    wgu2, wd2 = ffn2_wgu.astype(BF16), ffn2_wd.astype(BF16)
    wqkv, wo_a = moba_wqkv.astype(BF16), moba_wo.astype(BF16)
    wo_b = mla_wo.astype(BF16)
    slopes = _alibi_slopes(H)
    cs = _rope_table(S)

    kr = mla_wdkv[:, KV_LORA:]
    wdkv = jnp.concatenate([mla_wdkv, _rotate_half_cols(kr)], axis=-1).astype(BF16)
    wukv = mla_wukv.astype(BF16)
    wdq = mla_wdq.astype(BF16)
    R = mla_wuq.shape[1]
    wuq4 = mla_wuq.reshape(n_b, R, H, QK_NOPE + QK_ROPE)
    rope_cols = wuq4[..., QK_NOPE:]
    wuq = jnp.concatenate([wuq4, _rotate_half_cols(rope_cols)], axis=-1)
    wuq = wuq.reshape(n_b, R, H * HEAD_PAD).astype(BF16)
    cqn = cq_norm.reshape(n_b, 1, R)

    h = x.reshape(S, D)

    def macaron(h, l, mixer):
        h = _ffn(h, ln1, wgu1, wd1, l)
        h = mixer(h)
        return _ffn(h, ln2, wgu2, wd2, l)

    for l in range(n_a):
        def moba(h, l=l):
            qkv = _moba_qkv(h, lnm, wqkv, l)
            o = _moba_attention(qkv, slopes)
            return _oproj(h, o, wo_a, l)
        h = macaron(h, l, moba)

    k_sh, v_sh = _mla_kv(h, kv_norm.reshape(1, D), wdkv, ckv_norm.reshape(1, KV_LORA), wukv, cs)

    for j in range(n_b):
        l = n_a + j

        def mla(h, l=l, j=j):
            q = _mla_q(h, lnm, wdq, cqn, wuq, cs, l, j)
            o = _mla_attention(q, k_sh, v_sh)
            return _oproj(h, o, wo_b, j)
        h = macaron(h, l, mla)

    return _final_norm(h, final_norm.reshape(1, D)).reshape(B, S, D)
```

```python
import functools

import numpy as np
import jax
import jax.numpy as jnp
from jax import lax
from jax.experimental import pallas as pl
from jax.experimental.pallas import tpu as pltpu

F32 = jnp.float32
BF16 = jnp.bfloat16

EPS = 1e-6
ROPE_THETA = 10000.0

N_HEADS = 16
MOBA_BLOCK = 256
MOBA_TOPK = 3
KV_LORA = 512
QK_NOPE = 128
QK_ROPE = 64
V_DIM = 128

LANES = 128
HEAD_PAD = 2 * LANES
MASK_BIAS = -1e9
M_INIT = -1e30
LOG2E = 1.4426950408889634
VMEM_LIMIT = 56 * 1024 * 1024

ATTN_TQ = 1024
ATTN_TK = 512
ATTN_ROWS = 256

_ARB1 = ("arbitrary",)
_ARB2 = ("arbitrary", "arbitrary")


def _params(sem):
    return pltpu.CompilerParams(dimension_semantics=sem, vmem_limit_bytes=VMEM_LIMIT)


def _rms(x, g):
    ms = jnp.mean(x * x, axis=-1, keepdims=True)
    return x * lax.rsqrt(ms + EPS) * g


def _dot(a, b):
    return jnp.dot(a, b, preferred_element_type=F32)


def _dot_nt(a, b):
    return lax.dot_general(a, b, (((1,), (1,)), ((), ())), preferred_element_type=F32)


def _ffn_kernel(h_ref, g_ref, wg_ref, wu_ref, wd_ref, o_ref, xn_ref, acc_ref, *, nf):
    f = pl.program_id(1)

    @pl.when(f == 0)
    def _():
        xn_ref[...] = _rms(h_ref[...], g_ref[...]).astype(BF16)
        acc_ref[...] = jnp.zeros_like(acc_ref)

    xn = xn_ref[...]
    gate = _dot(xn, wg_ref[...])
    up = _dot(xn, wu_ref[...])
    act = (gate * jax.nn.sigmoid(gate) * up).astype(BF16)
    acc_ref[...] += _dot(act, wd_ref[...])

    @pl.when(f == nf - 1)
    def _():
        o_ref[...] = h_ref[...] + 0.5 * acc_ref[...]


def _ffn(h, ln, wgu, wd, layer, *, tm=512, tf=512):
    S, D = h.shape
    F = wd.shape[1]
    nf = F // tf
    return pl.pallas_call(
        functools.partial(_ffn_kernel, nf=nf),
        grid=(S // tm, nf),
        in_specs=[
            pl.BlockSpec((tm, D), lambda i, f: (i, 0)),
            pl.BlockSpec((None, 1, D), lambda i, f: (layer, 0, 0)),
            pl.BlockSpec((None, D, tf), lambda i, f: (layer, 0, f)),
            pl.BlockSpec((None, D, tf), lambda i, f: (layer, 0, f + nf)),
            pl.BlockSpec((None, tf, D), lambda i, f: (layer, f, 0)),
        ],
        out_specs=pl.BlockSpec((tm, D), lambda i, f: (i, 0)),
        out_shape=jax.ShapeDtypeStruct((S, D), F32),
        scratch_shapes=[pltpu.VMEM((tm, D), BF16), pltpu.VMEM((tm, D), F32)],
        compiler_params=_params(_ARB2),
        name="ffn",
    )(h, ln, wgu, wgu, wd)


def _qkv_kernel(h_ref, g_ref, w_ref, o_ref, xn_ref, *, n_q_blocks, scale):
    j = pl.program_id(1)

    @pl.when(j == 0)
    def _():
        xn_ref[...] = _rms(h_ref[...], g_ref[...]).astype(BF16)

    acc = _dot(xn_ref[...], w_ref[...])
    o_ref[...] = (acc * jnp.where(j < n_q_blocks, scale, 1.0)).astype(BF16)


def _moba_qkv(h, ln, w, layer, *, tm=512, tn=512):
    S, D = h.shape
    N = w.shape[2]
    dh = D // N_HEADS
    return pl.pallas_call(
        functools.partial(_qkv_kernel, n_q_blocks=D // tn, scale=LOG2E * float(dh) ** -0.5),
        grid=(S // tm, N // tn),
        in_specs=[
            pl.BlockSpec((tm, D), lambda i, j: (i, 0)),
            pl.BlockSpec((None, 1, D), lambda i, j: (layer, 0, 0)),
            pl.BlockSpec((None, D, tn), lambda i, j: (layer, 0, j)),
        ],
        out_specs=pl.BlockSpec((tm, tn), lambda i, j: (i, j)),
        out_shape=jax.ShapeDtypeStruct((S, N), BF16),
        scratch_shapes=[pltpu.VMEM((tm, D), BF16)],
        compiler_params=_params(_ARB2),
        name="moba_qkv",
    )(h, ln, w)


def _tile_update(s_ref, v, m_ref, acc_ref, p_ref, r0, *, bias_ref=None, shift=None, causal=False):
    rows, tk = s_ref.shape[0] - r0, s_ref.shape[1]
    for c0 in range(r0, r0 + rows, ATTN_ROWS):
        c1 = c0 + ATTN_ROWS
        s = s_ref[c0:c1, :]
        if bias_ref is not None:
            s = s + bias_ref[c0 - r0:c1 - r0, :]
        if causal:
            qpos = lax.broadcasted_iota(jnp.int32, s.shape, 0) + (c0 - r0)
            kpos = lax.broadcasted_iota(jnp.int32, s.shape, 1)
            s = jnp.where(kpos <= qpos, s, -jnp.inf)
        m_old = m_ref[c0:c1, :]
        row_max = jnp.max(s, axis=-1, keepdims=True)
        if shift is not None:
            row_max = row_max - shift
        m_new = jnp.maximum(m_old, row_max)
        m_ref[c0:c1, :] = m_new
        sub = m_new if shift is None else m_new + shift
        p_ref[c0:c1, :] = jnp.exp2(s - jnp.tile(sub, (1, tk // LANES))).astype(BF16)
        alpha = jnp.exp2(m_old - m_new)
        acc_ref[c0:c1, :] = jnp.tile(alpha, (1, acc_ref.shape[1] // LANES)) * acc_ref[c0:c1, :]
    acc_ref[r0:, :] += _dot(p_ref[r0:, :], v)


def _causal_attention(qi, score_fn, va_ref, s_refs, m_ref, acc_ref, p_ref, o_ref,
                      *, tq, tk, bias_ref=None, slope2=None):
    per = tq // tk
    assert per % 2 == 0
    dv = o_ref.shape[1]
    m_ref[...] = jnp.full_like(m_ref, M_INIT)
    acc_ref[...] = jnp.zeros_like(acc_ref)
    s_refs[0][...] = score_fn(0, 0)

    def tile(t, rd, wr):
        start = pl.multiple_of(t * tk, tk)
        wr[...] = score_fn(0, start + tk)
        shift = None if slope2 is None else slope2 * (qi * tq - t * tk).astype(F32)
        _tile_update(rd, va_ref[pl.ds(start, tk), :], m_ref, acc_ref, p_ref, 0,
                     bias_ref=bias_ref, shift=shift)

    def pair(j, carry):
        tile(2 * j, s_refs[0], s_refs[1])
        tile(2 * j + 1, s_refs[1], s_refs[0])
        return carry

    lax.fori_loop(0, qi * (per // 2), pair, 0)

    for b in range(per):
        r0 = b * tk
        start = pl.multiple_of(qi * tq + r0, tk)
        rd, wr = s_refs[b % 2], s_refs[1 - b % 2]
        if b + 1 < per:
            wr[r0 + tk:, :] = score_fn(r0 + tk, start + tk)
        _tile_update(rd, va_ref[pl.ds(start, tk), :], m_ref, acc_ref, p_ref, r0,
                     bias_ref=bias_ref, causal=True)

    o_ref[...] = (acc_ref[:, :dv] / acc_ref[:, dv:]).astype(BF16)


def _attn_scratch(S, tq, tk, dv):
    return [
        pltpu.VMEM((S, dv + LANES), BF16),
        pltpu.VMEM((tq, tk), F32),
        pltpu.VMEM((tq, tk), F32),
        pltpu.VMEM((tq, LANES), F32),
        pltpu.VMEM((tq, dv + LANES), F32),
        pltpu.VMEM((tq, tk), BF16),
    ]


def _fill_values(va_ref, v_ref):
    dv = v_ref.shape[1]
    va_ref[:, :dv] = v_ref[...]
    va_ref[:, dv:] = jnp.ones((va_ref.shape[0], va_ref.shape[1] - dv), BF16)


def _moba_kernel(slopes_ref, q_ref, k_ref, v_ref, o_ref,
                 kaug_ref, kmean_ref, bias_ref, qaug_ref, va_ref, s0_ref, s1_ref, m_ref, acc_ref, p_ref,
                 *, nb, bs, topk, tq, tk):
    h = pl.program_id(0)
    qi = pl.program_id(1)
    slope2 = slopes_ref[h] * LOG2E
    dh = q_ref.shape[1]

    @pl.when(qi == 0)
    def _():
        _fill_values(va_ref, v_ref)
        kaug_ref[:, :dh] = k_ref[...]
        lane_b = lax.broadcasted_iota(jnp.int32, (bs, LANES), 1)
        kmean_ref[...] = jnp.zeros_like(kmean_ref)
        for n in range(nb):
            kaug_ref[n * bs:(n + 1) * bs, dh:] = (lane_b == n).astype(BF16)
            kmean_ref[n:n + 1, :] = jnp.mean(
                k_ref[n * bs:(n + 1) * bs, :].astype(F32), axis=0, keepdims=True)
        qpos = lax.broadcasted_iota(jnp.int32, (tq, tk), 0)
        kpos = lax.broadcasted_iota(jnp.int32, (tq, tk), 1)
        bias_ref[...] = -slope2 * (qpos - kpos).astype(F32)

    q = q_ref[...]

    gate = _dot_nt(q, kmean_ref[...].astype(BF16))
    lane = lax.broadcasted_iota(jnp.int32, (tq, LANES), 1)
    cur = qi * (tq // bs) + lax.broadcasted_iota(jnp.int32, (tq, LANES), 0) // bs
    valid = lane < cur
    g = jnp.where(valid, gate, -jnp.inf)
    picked = jnp.zeros((tq, LANES), jnp.bool_)
    for _ in range(topk):
        mx = jnp.max(g, axis=-1, keepdims=True)
        idx = jnp.min(jnp.where(g == mx, lane, LANES), axis=-1, keepdims=True)
        hit = lane == idx
        picked = jnp.logical_or(picked, hit)
        g = jnp.where(hit, -jnp.inf, g)
    attend = jnp.logical_or(jnp.logical_and(picked, valid), lane == cur)
    selbias = jnp.where(attend, 0.0, MASK_BIAS)
    qaug_ref[...] = jnp.concatenate([q, selbias.astype(BF16)], axis=1)

    def score_fn(r0, start):
        return _dot_nt(qaug_ref[r0:, :], kaug_ref[pl.ds(start, tk), :])

    _causal_attention(qi, score_fn, va_ref, (s0_ref, s1_ref), m_ref, acc_ref, p_ref, o_ref,
                      tq=tq, tk=tk, bias_ref=bias_ref, slope2=slope2)


def _moba_attention(qkv, slopes, *, tq=ATTN_TQ, tk=ATTN_TK):
    S, N = qkv.shape
    H = N_HEADS
    dh = N // (3 * H)
    bs = MOBA_BLOCK
    nb = S // bs
    assert S % tq == 0 and tq % tk == 0 and tk % bs == 0 and nb <= LANES and dh == LANES
    return pl.pallas_call(
        functools.partial(_moba_kernel, nb=nb, bs=bs, topk=min(MOBA_TOPK, nb), tq=tq, tk=tk),
        grid=(H, S // tq),
        in_specs=[
            pl.BlockSpec(memory_space=pltpu.SMEM),
            pl.BlockSpec((tq, dh), lambda h, i: (i, h)),
            pl.BlockSpec((S, dh), lambda h, i: (0, H + h)),
            pl.BlockSpec((S, dh), lambda h, i: (0, 2 * H + h)),
        ],
        out_specs=pl.BlockSpec((tq, dh), lambda h, i: (i, h)),
        out_shape=jax.ShapeDtypeStruct((S, H * dh), BF16),
        scratch_shapes=[
            pltpu.VMEM((S, dh + LANES), BF16),
            pltpu.VMEM((LANES, dh), F32),
            pltpu.VMEM((tq, tk), F32),
            pltpu.VMEM((tq, dh + LANES), BF16),
        ] + _attn_scratch(S, tq, tk, dh),
        compiler_params=_params(_ARB2),
        name="moba_attn",
    )(slopes, qkv, qkv, qkv)


def _oproj_kernel(h_ref, o_ref, w_ref, out_ref):
    out_ref[...] = h_ref[...] + _dot(o_ref[...], w_ref[...])


def _oproj(h, o, w, layer, *, tm=512):
    S, D = h.shape
    K = o.shape[1]
    return pl.pallas_call(
        _oproj_kernel,
        grid=(S // tm,),
        in_specs=[
            pl.BlockSpec((tm, D), lambda i: (i, 0)),
            pl.BlockSpec((tm, K), lambda i: (i, 0)),
            pl.BlockSpec((None, K, D), lambda i: (layer, 0, 0)),
        ],
        out_specs=pl.BlockSpec((tm, D), lambda i: (i, 0)),
        out_shape=jax.ShapeDtypeStruct((S, D), F32),
        compiler_params=_params(_ARB1),
        name="oproj",
    )(h, o, w)


def _rope_pair(pair, cs):
    y = pair * cs
    z = y + pltpu.roll(y, LANES // 2, 1)
    lane = lax.broadcasted_iota(jnp.int32, y.shape, 1)
    return jnp.where(lane < LANES // 2, z, 0.0)


def _mla_kv_kernel(h_ref, g_ref, wd_ref, cg_ref, wu_ref, cs_ref, k_ref, v_ref, *, nh):
    hn = _rms(h_ref[...], g_ref[...]).astype(BF16)
    t = _dot(hn, wd_ref[...])
    c = _rms(t[:, :KV_LORA], cg_ref[...]).astype(BF16)
    k_rope = _rope_pair(t[:, KV_LORA:], cs_ref[...])
    for hd in range(nh):
        a = _dot(c, wu_ref[:, hd * (QK_NOPE + V_DIM):(hd + 1) * (QK_NOPE + V_DIM)])
        k_ref[hd] = jnp.concatenate([a[:, :QK_NOPE], k_rope], axis=1).astype(BF16)
        v_ref[hd] = a[:, QK_NOPE:].astype(BF16)


def _mla_kv(h, g, wd, cg, wu, cs, *, tm=256):
    S, D = h.shape
    H = N_HEADS
    return pl.pallas_call(
        functools.partial(_mla_kv_kernel, nh=H),
        grid=(S // tm,),
        in_specs=[
            pl.BlockSpec((tm, D), lambda i: (i, 0)),
            pl.BlockSpec((1, D), lambda i: (0, 0)),
            pl.BlockSpec(wd.shape, lambda i: (0, 0)),
            pl.BlockSpec((1, KV_LORA), lambda i: (0, 0)),
            pl.BlockSpec(wu.shape, lambda i: (0, 0)),
            pl.BlockSpec((tm, LANES), lambda i: (i, 0)),
        ],
        out_specs=[
            pl.BlockSpec((H, tm, HEAD_PAD), lambda i: (0, i, 0)),
            pl.BlockSpec((H, tm, V_DIM), lambda i: (0, i, 0)),
        ],
        out_shape=[
            jax.ShapeDtypeStruct((H, S, HEAD_PAD), BF16),
            jax.ShapeDtypeStruct((H, S, V_DIM), BF16),
        ],
        compiler_params=_params(_ARB1),
        name="mla_kv",
    )(h, g, wd, cg, wu, cs)


def _mla_q_kernel(h_ref, g_ref, wd_ref, cg_ref, wu_ref, cs_ref, q_ref, *, nh, scale):
    xn = _rms(h_ref[...], g_ref[...]).astype(BF16)
    cq = _rms(_dot(xn, wd_ref[...]), cg_ref[...]).astype(BF16)
    cs = cs_ref[...]
    for hd in range(nh):
        a = _dot(cq, wu_ref[:, hd * HEAD_PAD:(hd + 1) * HEAD_PAD])
        q_rope = _rope_pair(a[:, QK_NOPE:], cs)
        q_ref[hd] = (jnp.concatenate([a[:, :QK_NOPE], q_rope], axis=1) * scale).astype(BF16)


def _mla_q(h, ln, wd, cg, wu, cs, layer, j, *, tm=256):
    S, D = h.shape
    H = N_HEADS
    R = wd.shape[2]
    return pl.pallas_call(
        functools.partial(_mla_q_kernel, nh=H, scale=LOG2E * float(QK_NOPE + QK_ROPE) ** -0.5),
        grid=(S // tm,),
        in_specs=[
            pl.BlockSpec((tm, D), lambda i: (i, 0)),
            pl.BlockSpec((None, 1, D), lambda i: (layer, 0, 0)),
            pl.BlockSpec((None, D, R), lambda i: (j, 0, 0)),
            pl.BlockSpec((None, 1, R), lambda i: (j, 0, 0)),
            pl.BlockSpec((None, R, H * HEAD_PAD), lambda i: (j, 0, 0)),
            pl.BlockSpec((tm, LANES), lambda i: (i, 0)),
        ],
        out_specs=pl.BlockSpec((H, tm, HEAD_PAD), lambda i: (0, i, 0)),
        out_shape=jax.ShapeDtypeStruct((H, S, HEAD_PAD), BF16),
        compiler_params=_params(_ARB1),
        name="mla_q",
    )(h, ln, wd, cg, wu, cs)


def _mla_attn_kernel(q_ref, k_ref, v_ref, o_ref, va_ref, s0_ref, s1_ref, m_ref, acc_ref, p_ref,
                     *, tq, tk):
    qi = pl.program_id(1)

    @pl.when(qi == 0)
    def _():
        _fill_values(va_ref, v_ref)

    def score_fn(r0, start):
        return _dot_nt(q_ref[r0:, :], k_ref[pl.ds(start, tk), :])

    _causal_attention(qi, score_fn, va_ref, (s0_ref, s1_ref), m_ref, acc_ref, p_ref, o_ref,
                      tq=tq, tk=tk)


def _mla_attention(q, k, v, *, tq=ATTN_TQ, tk=ATTN_TK):
    H, S, _ = q.shape
    assert S % tq == 0 and tq % tk == 0
    return pl.pallas_call(
        functools.partial(_mla_attn_kernel, tq=tq, tk=tk),
        grid=(H, S // tq),
        in_specs=[
            pl.BlockSpec((None, tq, HEAD_PAD), lambda h, i: (h, i, 0)),
            pl.BlockSpec((None, S, HEAD_PAD), lambda h, i: (h, 0, 0)),
            pl.BlockSpec((None, S, V_DIM), lambda h, i: (h, 0, 0)),
        ],
        out_specs=pl.BlockSpec((tq, V_DIM), lambda h, i: (i, h)),
        out_shape=jax.ShapeDtypeStruct((S, H * V_DIM), BF16),
        scratch_shapes=_attn_scratch(S, tq, tk, V_DIM),
        compiler_params=_params(_ARB2),
        name="mla_attn",
    )(q, k, v)


def _norm_kernel(h_ref, g_ref, o_ref):
    o_ref[...] = _rms(h_ref[...], g_ref[...])


def _final_norm(h, g, *, tm=512):
    S, D = h.shape
    return pl.pallas_call(
        _norm_kernel,
        grid=(S // tm,),
        in_specs=[pl.BlockSpec((tm, D), lambda i: (i, 0)), pl.BlockSpec((1, D), lambda i: (0, 0))],
        out_specs=pl.BlockSpec((tm, D), lambda i: (i, 0)),
        out_shape=jax.ShapeDtypeStruct((S, D), F32),
        compiler_params=_params(_ARB1),
        name="final_norm",
    )(h, g)


def _rotate_half_cols(w):
    w1, w2 = jnp.split(w, 2, axis=-1)
    return jnp.concatenate([-w2, w1], axis=-1)


def _alibi_slopes(n):
    return jnp.asarray(np.array([2.0 ** (-8.0 * (i + 1) / n) for i in range(n)], dtype=np.float32))


def _rope_table(s):
    inv = 1.0 / (ROPE_THETA ** (jnp.arange(0, QK_ROPE, 2, dtype=F32) / QK_ROPE))
    ang = jnp.arange(s, dtype=F32)[:, None] * inv[None, :]
    cos, sin = jnp.cos(ang), jnp.sin(ang)
    return jnp.concatenate([cos, cos, sin, sin], axis=-1)


def kernel(x, ln_ffn1, ffn1_wgu, ffn1_wd, ln_mix, ln_ffn2, ffn2_wgu, ffn2_wd, moba_wqkv, moba_wo, kv_norm, mla_wdkv, ckv_norm, mla_wukv, mla_wdq, cq_norm, mla_wuq, mla_wo, final_norm):
    B, S, D = x.shape
    assert B == 1
    depth = ln_ffn1.shape[0]
    n_a = moba_wqkv.shape[0]
    n_b = mla_wdq.shape[0]
    H = N_HEADS

    ln1 = ln_ffn1.reshape(depth, 1, D)
    lnm = ln_mix.reshape(depth, 1, D)
    ln2 = ln_ffn2.reshape(depth, 1, D)
    wgu1, wd1 = ffn1_wgu.astype(BF16), ffn1_wd.astype(BF16)
    wgu2, wd2 = ffn2_wgu.astype(BF16), ffn2_wd.astype(BF16)
    wqkv, wo_a = moba_wqkv.astype(BF16), moba_wo.astype(BF16)
    wo_b = mla_wo.astype(BF16)
    slopes = _alibi_slopes(H)
    cs = _rope_table(S)

    kr = mla_wdkv[:, KV_LORA:]
    wdkv = jnp.concatenate([mla_wdkv, _rotate_half_cols(kr)], axis=-1).astype(BF16)
    wukv = mla_wukv.astype(BF16)
    wdq = mla_wdq.astype(BF16)
    R = mla_wuq.shape[1]
    wuq4 = mla_wuq.reshape(n_b, R, H, QK_NOPE + QK_ROPE)
    rope_cols = wuq4[..., QK_NOPE:]
    wuq = jnp.concatenate([wuq4, _rotate_half_cols(rope_cols)], axis=-1)
    wuq = wuq.reshape(n_b, R, H * HEAD_PAD).astype(BF16)
    cqn = cq_norm.reshape(n_b, 1, R)

    h = x.reshape(S, D)

    def macaron(h, l, mixer):
        h = _ffn(h, ln1, wgu1, wd1, l)
        h = mixer(h)
        return _ffn(h, ln2, wgu2, wd2, l)

    for l in range(n_a):
        def moba(h, l=l):
            qkv = _moba_qkv(h, lnm, wqkv, l)
            o = _moba_attention(qkv, slopes)
            return _oproj(h, o, wo_a, l)
        h = macaron(h, l, moba)

    k_sh, v_sh = _mla_kv(h, kv_norm.reshape(1, D), wdkv, ckv_norm.reshape(1, KV_LORA), wukv, cs)

    for j in range(n_b):
        l = n_a + j

        def mla(h, l=l, j=j):
            q = _mla_q(h, lnm, wdq, cqn, wuq, cs, l, j)
            o = _mla_attention(q, k_sh, v_sh)
            return _oproj(h, o, wo_b, j)
        h = macaron(h, l, mla)

    return _final_norm(h, final_norm.reshape(1, D)).reshape(B, S, D)
```

```python
import functools

import numpy as np
import jax
import jax.numpy as jnp
from jax import lax
from jax.experimental import pallas as pl
from jax.experimental.pallas import tpu as pltpu

F32 = jnp.float32
BF16 = jnp.bfloat16

EPS = 1e-6
ROPE_THETA = 10000.0

N_HEADS = 16
MOBA_BLOCK = 256
MOBA_TOPK = 3
KV_LORA = 512
QK_NOPE = 128
QK_ROPE = 64
V_DIM = 128

LANES = 128
HEAD_PAD = 2 * LANES
MASK_BIAS = -1e9
M_INIT = -1e30
LOG2E = 1.4426950408889634
VMEM_LIMIT = 56 * 1024 * 1024

ATTN_TQ = 1024
ATTN_TK = 512
ATTN_ROWS = 256

_ARB1 = ("arbitrary",)
_ARB2 = ("arbitrary", "arbitrary")


def _params(sem):
    return pltpu.CompilerParams(dimension_semantics=sem, vmem_limit_bytes=VMEM_LIMIT)


def _rms(x, g):
    ms = jnp.mean(x * x, axis=-1, keepdims=True)
    return x * lax.rsqrt(ms + EPS) * g


def _dot(a, b):
    return jnp.dot(a, b, preferred_element_type=F32)


def _dot_nt(a, b):
    return lax.dot_general(a, b, (((1,), (1,)), ((), ())), preferred_element_type=F32)


def _ffn_kernel(h_ref, g_ref, wg_ref, wu_ref, wd_ref, o_ref, xn_ref, acc_ref, *, nf):
    f = pl.program_id(1)

    @pl.when(f == 0)
    def _():
        xn_ref[...] = _rms(h_ref[...], g_ref[...]).astype(BF16)
        acc_ref[...] = jnp.zeros_like(acc_ref)

    xn = xn_ref[...]
    gate = _dot(xn, wg_ref[...])
    up = _dot(xn, wu_ref[...])
    act = (gate * jax.nn.sigmoid(gate) * up).astype(BF16)
    acc_ref[...] += _dot(act, wd_ref[...])

    @pl.when(f == nf - 1)
    def _():
        o_ref[...] = h_ref[...] + 0.5 * acc_ref[...]


def _ffn(h, ln, wgu, wd, layer, *, tm=512, tf=512):
    S, D = h.shape
    F = wd.shape[1]
    nf = F // tf
    return pl.pallas_call(
        functools.partial(_ffn_kernel, nf=nf),
        grid=(S // tm, nf),
        in_specs=[
            pl.BlockSpec((tm, D), lambda i, f: (i, 0)),
            pl.BlockSpec((None, 1, D), lambda i, f: (layer, 0, 0)),
            pl.BlockSpec((None, D, tf), lambda i, f: (layer, 0, f)),
            pl.BlockSpec((None, D, tf), lambda i, f: (layer, 0, f + nf)),
            pl.BlockSpec((None, tf, D), lambda i, f: (layer, f, 0)),
        ],
        out_specs=pl.BlockSpec((tm, D), lambda i, f: (i, 0)),
        out_shape=jax.ShapeDtypeStruct((S, D), F32),
        scratch_shapes=[pltpu.VMEM((tm, D), BF16), pltpu.VMEM((tm, D), F32)],
        compiler_params=_params(_ARB2),
        name="ffn",
    )(h, ln, wgu, wgu, wd)


def _qkv_kernel(h_ref, g_ref, w_ref, o_ref, xn_ref, *, n_q_blocks, scale):
    j = pl.program_id(1)

    @pl.when(j == 0)
    def _():
        xn_ref[...] = _rms(h_ref[...], g_ref[...]).astype(BF16)

    acc = _dot(xn_ref[...], w_ref[...])
    o_ref[...] = (acc * jnp.where(j < n_q_blocks, scale, 1.0)).astype(BF16)


def _moba_qkv(h, ln, w, layer, *, tm=1024, tn=1024):
    S, D = h.shape
    N = w.shape[2]
    dh = D // N_HEADS
    return pl.pallas_call(
        functools.partial(_qkv_kernel, n_q_blocks=D // tn, scale=LOG2E * float(dh) ** -0.5),
        grid=(S // tm, N // tn),
        in_specs=[
            pl.BlockSpec((tm, D), lambda i, j: (i, 0)),
            pl.BlockSpec((None, 1, D), lambda i, j: (layer, 0, 0)),
            pl.BlockSpec((None, D, tn), lambda i, j: (layer, 0, j)),
        ],
        out_specs=pl.BlockSpec((tm, tn), lambda i, j: (i, j)),
        out_shape=jax.ShapeDtypeStruct((S, N), BF16),
        scratch_shapes=[pltpu.VMEM((tm, D), BF16)],
        compiler_params=_params(_ARB2),
        name="moba_qkv",
    )(h, ln, w)


def _tile_update(s_ref, v, m_ref, acc_ref, p_ref, r0, *, bias_ref=None, shift=None, causal=False):
    rows, tk = s_ref.shape[0] - r0, s_ref.shape[1]
    for c0 in range(r0, r0 + rows, ATTN_ROWS):
        c1 = c0 + ATTN_ROWS
        s = s_ref[c0:c1, :]
        if bias_ref is not None:
            s = s + bias_ref[c0 - r0:c1 - r0, :]
        if causal:
            qpos = lax.broadcasted_iota(jnp.int32, s.shape, 0) + (c0 - r0)
            kpos = lax.broadcasted_iota(jnp.int32, s.shape, 1)
            s = jnp.where(kpos <= qpos, s, -jnp.inf)
        m_old = m_ref[c0:c1, :]
        row_max = jnp.max(s, axis=-1, keepdims=True)
        if shift is not None:
            row_max = row_max - shift
        m_new = jnp.maximum(m_old, row_max)
        m_ref[c0:c1, :] = m_new
        sub = m_new if shift is None else m_new + shift
        p_ref[c0:c1, :] = jnp.exp2(s - jnp.tile(sub, (1, tk // LANES))).astype(BF16)
        alpha = jnp.exp2(m_old - m_new)
        acc_ref[c0:c1, :] = jnp.tile(alpha, (1, acc_ref.shape[1] // LANES)) * acc_ref[c0:c1, :]
    acc_ref[r0:, :] += _dot(p_ref[r0:, :], v)


def _causal_attention(qi, score_fn, va_ref, s_refs, m_ref, acc_ref, p_ref, o_ref,
                      *, tq, tk, bias_ref=None, slope2=None):
    per = tq // tk
    assert per % 2 == 0
    dv = o_ref.shape[1]
    m_ref[...] = jnp.full_like(m_ref, M_INIT)
    acc_ref[...] = jnp.zeros_like(acc_ref)
    s_refs[0][...] = score_fn(0, 0)

    def tile(t, rd, wr):
        start = pl.multiple_of(t * tk, tk)
        wr[...] = score_fn(0, start + tk)
        shift = None if slope2 is None else slope2 * (qi * tq - t * tk).astype(F32)
        _tile_update(rd, va_ref[pl.ds(start, tk), :], m_ref, acc_ref, p_ref, 0,
                     bias_ref=bias_ref, shift=shift)

    def pair(j, carry):
        tile(2 * j, s_refs[0], s_refs[1])
        tile(2 * j + 1, s_refs[1], s_refs[0])
        return carry

    lax.fori_loop(0, qi * (per // 2), pair, 0)

    for b in range(per):
        r0 = b * tk
        start = pl.multiple_of(qi * tq + r0, tk)
        rd, wr = s_refs[b % 2], s_refs[1 - b % 2]
        if b + 1 < per:
            wr[r0 + tk:, :] = score_fn(r0 + tk, start + tk)
        _tile_update(rd, va_ref[pl.ds(start, tk), :], m_ref, acc_ref, p_ref, r0,
                     bias_ref=bias_ref, causal=True)

    o_ref[...] = (acc_ref[:, :dv] / acc_ref[:, dv:]).astype(BF16)


def _attn_scratch(S, tq, tk, dv):
    return [
        pltpu.VMEM((S, dv + LANES), BF16),
        pltpu.VMEM((tq, tk), F32),
        pltpu.VMEM((tq, tk), F32),
        pltpu.VMEM((tq, LANES), F32),
        pltpu.VMEM((tq, dv + LANES), F32),
        pltpu.VMEM((tq, tk), BF16),
    ]


def _fill_values(va_ref, v_ref):
    dv = v_ref.shape[1]
    va_ref[:, :dv] = v_ref[...]
    va_ref[:, dv:] = jnp.ones((va_ref.shape[0], va_ref.shape[1] - dv), BF16)


def _moba_kernel(slopes_ref, q_ref, k_ref, v_ref, o_ref,
                 kaug_ref, kmean_ref, bias_ref, qaug_ref, va_ref, s0_ref, s1_ref, m_ref, acc_ref, p_ref,
                 *, nb, bs, topk, tq, tk):
    h = pl.program_id(0)
    qi = pl.program_id(1)
    slope2 = slopes_ref[h] * LOG2E
    dh = q_ref.shape[1]

    @pl.when(qi == 0)
    def _():
        _fill_values(va_ref, v_ref)
        kaug_ref[:, :dh] = k_ref[...]
        lane_b = lax.broadcasted_iota(jnp.int32, (bs, LANES), 1)
        kmean_ref[...] = jnp.zeros_like(kmean_ref)
        for n in range(nb):
            kaug_ref[n * bs:(n + 1) * bs, dh:] = (lane_b == n).astype(BF16)
            kmean_ref[n:n + 1, :] = jnp.mean(
                k_ref[n * bs:(n + 1) * bs, :].astype(F32), axis=0, keepdims=True)
        qpos = lax.broadcasted_iota(jnp.int32, (tq, tk), 0)
        kpos = lax.broadcasted_iota(jnp.int32, (tq, tk), 1)
        bias_ref[...] = -slope2 * (qpos - kpos).astype(F32)

    q = q_ref[...]

    nbp = -(-nb // 8) * 8
    gate = _dot_nt(kmean_ref[...].astype(BF16), q)[:nbp, :]
    blk = lax.broadcasted_iota(jnp.int32, (nbp, tq), 0)
    cur = qi * (tq // bs) + lax.broadcasted_iota(jnp.int32, (nbp, tq), 1) // bs
    valid = blk < cur
    blk_f = blk.astype(F32)
    g = jnp.where(valid, gate, -jnp.inf)
    picked = jnp.zeros((nbp, tq), jnp.bool_)
    for _ in range(topk):
        mx = jnp.max(g, axis=0, keepdims=True)
        first = jnp.min(jnp.where(g == mx, blk_f, float(nbp)), axis=0, keepdims=True)
        hit = blk_f == first
        picked = jnp.logical_or(picked, hit)
        g = jnp.where(hit, -jnp.inf, g)
    attend = jnp.logical_or(jnp.logical_and(picked, valid), blk == cur)
    selbias_t = jnp.concatenate(
        [jnp.where(attend, 0.0, MASK_BIAS), jnp.zeros((LANES - nbp, tq), F32)], axis=0)
    qaug_ref[...] = jnp.concatenate([q, selbias_t.T.astype(BF16)], axis=1)

    def score_fn(r0, start):
        return _dot_nt(qaug_ref[r0:, :], kaug_ref[pl.ds(start, tk), :])

    _causal_attention(qi, score_fn, va_ref, (s0_ref, s1_ref), m_ref, acc_ref, p_ref, o_ref,
                      tq=tq, tk=tk, bias_ref=bias_ref, slope2=slope2)


def _moba_attention(qkv, slopes, *, tq=ATTN_TQ, tk=ATTN_TK):
    S, N = qkv.shape
    H = N_HEADS
    dh = N // (3 * H)
    bs = MOBA_BLOCK
    nb = S // bs
    assert S % tq == 0 and tq % tk == 0 and tk % bs == 0 and nb <= LANES and dh == LANES
    return pl.pallas_call(
        functools.partial(_moba_kernel, nb=nb, bs=bs, topk=min(MOBA_TOPK, nb), tq=tq, tk=tk),
        grid=(H, S // tq),
        in_specs=[
            pl.BlockSpec(memory_space=pltpu.SMEM),
            pl.BlockSpec((tq, dh), lambda h, i: (i, h)),
            pl.BlockSpec((S, dh), lambda h, i: (0, H + h)),
            pl.BlockSpec((S, dh), lambda h, i: (0, 2 * H + h)),
        ],
        out_specs=pl.BlockSpec((tq, dh), lambda h, i: (i, h)),
        out_shape=jax.ShapeDtypeStruct((S, H * dh), BF16),
        scratch_shapes=[
            pltpu.VMEM((S, dh + LANES), BF16),
            pltpu.VMEM((LANES, dh), F32),
            pltpu.VMEM((tq, tk), F32),
            pltpu.VMEM((tq, dh + LANES), BF16),
        ] + _attn_scratch(S, tq, tk, dh),
        compiler_params=_params(_ARB2),
        name="moba_attn",
    )(slopes, qkv, qkv, qkv)


def _oproj_kernel(h_ref, o_ref, w_ref, out_ref):
    out_ref[...] = h_ref[...] + _dot(o_ref[...], w_ref[...])


def _oproj(h, o, w, layer, *, tm=512):
    S, D = h.shape
    K = o.shape[1]
    return pl.pallas_call(
        _oproj_kernel,
        grid=(S // tm,),
        in_specs=[
            pl.BlockSpec((tm, D), lambda i: (i, 0)),
            pl.BlockSpec((tm, K), lambda i: (i, 0)),
            pl.BlockSpec((None, K, D), lambda i: (layer, 0, 0)),
        ],
        out_specs=pl.BlockSpec((tm, D), lambda i: (i, 0)),
        out_shape=jax.ShapeDtypeStruct((S, D), F32),
        compiler_params=_params(_ARB1),
        name="oproj",
    )(h, o, w)


def _rope_pair(pair, cs):
    y = pair * cs
    z = y + pltpu.roll(y, LANES // 2, 1)
    lane = lax.broadcasted_iota(jnp.int32, y.shape, 1)
    return jnp.where(lane < LANES // 2, z, 0.0)


def _mla_kv_kernel(h_ref, g_ref, wd_ref, cg_ref, wu_ref, cs_ref, k_ref, v_ref, *, nh):
    hn = _rms(h_ref[...], g_ref[...]).astype(BF16)
    t = _dot(hn, wd_ref[...])
    c = _rms(t[:, :KV_LORA], cg_ref[...]).astype(BF16)
    k_rope = _rope_pair(t[:, KV_LORA:], cs_ref[...])
    for hd in range(nh):
        a = _dot(c, wu_ref[:, hd * (QK_NOPE + V_DIM):(hd + 1) * (QK_NOPE + V_DIM)])
        k_ref[hd] = jnp.concatenate([a[:, :QK_NOPE], k_rope], axis=1).astype(BF16)
        v_ref[hd] = a[:, QK_NOPE:].astype(BF16)


def _mla_kv(h, g, wd, cg, wu, cs, *, tm=256):
    S, D = h.shape
    H = N_HEADS
    return pl.pallas_call(
        functools.partial(_mla_kv_kernel, nh=H),
        grid=(S // tm,),
        in_specs=[
            pl.BlockSpec((tm, D), lambda i: (i, 0)),
            pl.BlockSpec((1, D), lambda i: (0, 0)),
            pl.BlockSpec(wd.shape, lambda i: (0, 0)),
            pl.BlockSpec((1, KV_LORA), lambda i: (0, 0)),
            pl.BlockSpec(wu.shape, lambda i: (0, 0)),
            pl.BlockSpec((tm, LANES), lambda i: (i, 0)),
        ],
        out_specs=[
            pl.BlockSpec((H, tm, HEAD_PAD), lambda i: (0, i, 0)),
            pl.BlockSpec((H, tm, V_DIM), lambda i: (0, i, 0)),
        ],
        out_shape=[
            jax.ShapeDtypeStruct((H, S, HEAD_PAD), BF16),
            jax.ShapeDtypeStruct((H, S, V_DIM), BF16),
        ],
        compiler_params=_params(_ARB1),
        name="mla_kv",
    )(h, g, wd, cg, wu, cs)


def _mla_q_kernel(h_ref, g_ref, wd_ref, cg_ref, wu_ref, cs_ref, q_ref, *, nh, scale):
    xn = _rms(h_ref[...], g_ref[...]).astype(BF16)
    cq = _rms(_dot(xn, wd_ref[...]), cg_ref[...]).astype(BF16)
    cs = cs_ref[...]
    for hd in range(nh):
        a = _dot(cq, wu_ref[:, hd * HEAD_PAD:(hd + 1) * HEAD_PAD])
        q_rope = _rope_pair(a[:, QK_NOPE:], cs)
        q_ref[hd] = (jnp.concatenate([a[:, :QK_NOPE], q_rope], axis=1) * scale).astype(BF16)


def _mla_q(h, ln, wd, cg, wu, cs, layer, j, *, tm=256):
    S, D = h.shape
    H = N_HEADS
    R = wd.shape[2]
    return pl.pallas_call(
        functools.partial(_mla_q_kernel, nh=H, scale=LOG2E * float(QK_NOPE + QK_ROPE) ** -0.5),
        grid=(S // tm,),
        in_specs=[
            pl.BlockSpec((tm, D), lambda i: (i, 0)),
            pl.BlockSpec((None, 1, D), lambda i: (layer, 0, 0)),
            pl.BlockSpec((None, D, R), lambda i: (j, 0, 0)),
            pl.BlockSpec((None, 1, R), lambda i: (j, 0, 0)),
            pl.BlockSpec((None, R, H * HEAD_PAD), lambda i: (j, 0, 0)),
            pl.BlockSpec((tm, LANES), lambda i: (i, 0)),
        ],
        out_specs=pl.BlockSpec((H, tm, HEAD_PAD), lambda i: (0, i, 0)),
        out_shape=jax.ShapeDtypeStruct((H, S, HEAD_PAD), BF16),
        compiler_params=_params(_ARB1),
        name="mla_q",
    )(h, ln, wd, cg, wu, cs)


def _mla_attn_kernel(q_ref, k_ref, v_ref, o_ref, va_ref, s0_ref, s1_ref, m_ref, acc_ref, p_ref,
                     *, tq, tk):
    qi = pl.program_id(1)

    @pl.when(qi == 0)
    def _():
        _fill_values(va_ref, v_ref)

    def score_fn(r0, start):
        return _dot_nt(q_ref[r0:, :], k_ref[pl.ds(start, tk), :])

    _causal_attention(qi, score_fn, va_ref, (s0_ref, s1_ref), m_ref, acc_ref, p_ref, o_ref,
                      tq=tq, tk=tk)


def _mla_attention(q, k, v, *, tq=ATTN_TQ, tk=ATTN_TK):
    H, S, _ = q.shape
    assert S % tq == 0 and tq % tk == 0
    return pl.pallas_call(
        functools.partial(_mla_attn_kernel, tq=tq, tk=tk),
        grid=(H, S // tq),
        in_specs=[
            pl.BlockSpec((None, tq, HEAD_PAD), lambda h, i: (h, i, 0)),
            pl.BlockSpec((None, S, HEAD_PAD), lambda h, i: (h, 0, 0)),
            pl.BlockSpec((None, S, V_DIM), lambda h, i: (h, 0, 0)),
        ],
        out_specs=pl.BlockSpec((tq, V_DIM), lambda h, i: (i, h)),
        out_shape=jax.ShapeDtypeStruct((S, H * V_DIM), BF16),
        scratch_shapes=_attn_scratch(S, tq, tk, V_DIM),
        compiler_params=_params(_ARB2),
        name="mla_attn",
    )(q, k, v)


def _norm_kernel(h_ref, g_ref, o_ref):
    o_ref[...] = _rms(h_ref[...], g_ref[...])


def _final_norm(h, g, *, tm=512):
    S, D = h.shape
    return pl.pallas_call(
        _norm_kernel,
        grid=(S // tm,),
        in_specs=[pl.BlockSpec((tm, D), lambda i: (i, 0)), pl.BlockSpec((1, D), lambda i: (0, 0))],
        out_specs=pl.BlockSpec((tm, D), lambda i: (i, 0)),
        out_shape=jax.ShapeDtypeStruct((S, D), F32),
        compiler_params=_params(_ARB1),
        name="final_norm",
    )(h, g)


def _rotate_half_cols(w):
    w1, w2 = jnp.split(w, 2, axis=-1)
    return jnp.concatenate([-w2, w1], axis=-1)


def _alibi_slopes(n):
    return jnp.asarray(np.array([2.0 ** (-8.0 * (i + 1) / n) for i in range(n)], dtype=np.float32))


def _rope_table(s):
    inv = 1.0 / (ROPE_THETA ** (jnp.arange(0, QK_ROPE, 2, dtype=F32) / QK_ROPE))
    ang = jnp.arange(s, dtype=F32)[:, None] * inv[None, :]
    cos, sin = jnp.cos(ang), jnp.sin(ang)
    return jnp.concatenate([cos, cos, sin, sin], axis=-1)


def kernel(x, ln_ffn1, ffn1_wgu, ffn1_wd, ln_mix, ln_ffn2, ffn2_wgu, ffn2_wd, moba_wqkv, moba_wo, kv_norm, mla_wdkv, ckv_norm, mla_wukv, mla_wdq, cq_norm, mla_wuq, mla_wo, final_norm):
    B, S, D = x.shape
    assert B == 1
    depth = ln_ffn1.shape[0]
    n_a = moba_wqkv.shape[0]
    n_b = mla_wdq.shape[0]
    H = N_HEADS

    ln1 = ln_ffn1.reshape(depth, 1, D)
    lnm = ln_mix.reshape(depth, 1, D)
    ln2 = ln_ffn2.reshape(depth, 1, D)
    wgu1, wd1 = ffn1_wgu.astype(BF16), ffn1_wd.astype(BF16)
    wgu2, wd2 = ffn2_wgu.astype(BF16), ffn2_wd.astype(BF16)
    wqkv, wo_a = moba_wqkv.astype(BF16), moba_wo.astype(BF16)
    wo_b = mla_wo.astype(BF16)
    slopes = _alibi_slopes(H)
    cs = _rope_table(S)

    kr = mla_wdkv[:, KV_LORA:]
    wdkv = jnp.concatenate([mla_wdkv, _rotate_half_cols(kr)], axis=-1).astype(BF16)
    wukv = mla_wukv.astype(BF16)
    wdq = mla_wdq.astype(BF16)
    R = mla_wuq.shape[1]
    wuq4 = mla_wuq.reshape(n_b, R, H, QK_NOPE + QK_ROPE)
    rope_cols = wuq4[..., QK_NOPE:]
    wuq = jnp.concatenate([wuq4, _rotate_half_cols(rope_cols)], axis=-1)
    wuq = wuq.reshape(n_b, R, H * HEAD_PAD).astype(BF16)
    cqn = cq_norm.reshape(n_b, 1, R)

    h = x.reshape(S, D)

    def macaron(h, l, mixer):
        h = _ffn(h, ln1, wgu1, wd1, l)
        h = mixer(h)
        return _ffn(h, ln2, wgu2, wd2, l)

    for l in range(n_a):
        def moba(h, l=l):
            qkv = _moba_qkv(h, lnm, wqkv, l)
            o = _moba_attention(qkv, slopes)
            return _oproj(h, o, wo_a, l)
        h = macaron(h, l, moba)

    k_sh, v_sh = _mla_kv(h, kv_norm.reshape(1, D), wdkv, ckv_norm.reshape(1, KV_LORA), wukv, cs)

    for j in range(n_b):
        l = n_a + j

        def mla(h, l=l, j=j):
            q = _mla_q(h, lnm, wdq, cqn, wuq, cs, l, j)
            o = _mla_attention(q, k_sh, v_sh)
            return _oproj(h, o, wo_b, j)
        h = macaron(h, l, mla)

    return _final_norm(h, final_norm.reshape(1, D)).reshape(B, S, D)
```

```python
import functools

import numpy as np
import jax
import jax.numpy as jnp
from jax import lax
from jax.experimental import pallas as pl
from jax.experimental.pallas import tpu as pltpu

F32 = jnp.float32
BF16 = jnp.bfloat16

EPS = 1e-6
ROPE_THETA = 10000.0

N_HEADS = 16
MOBA_BLOCK = 256
MOBA_TOPK = 3
KV_LORA = 512
QK_NOPE = 128
QK_ROPE = 64
V_DIM = 128

LANES = 128
HEAD_PAD = 2 * LANES
MASK_BIAS = -1e9
M_INIT = -1e30
LOG2E = 1.4426950408889634
VMEM_LIMIT = 56 * 1024 * 1024

ATTN_TQ = 1024
ATTN_TK = 512
ATTN_ROWS = 256

_ARB1 = ("arbitrary",)
_ARB2 = ("arbitrary", "arbitrary")


def _params(sem):
    return pltpu.CompilerParams(dimension_semantics=sem, vmem_limit_bytes=VMEM_LIMIT)


def _rms(x, g):
    ms = jnp.mean(x * x, axis=-1, keepdims=True)
    return x * lax.rsqrt(ms + EPS) * g


def _dot(a, b):
    return jnp.dot(a, b, preferred_element_type=F32)


def _dot_nt(a, b):
    return lax.dot_general(a, b, (((1,), (1,)), ((), ())), preferred_element_type=F32)


def _ffn_kernel(h_ref, g_ref, wg_ref, wu_ref, wd_ref, *rest, nf, final):
    fg_ref = rest[0] if final else None
    o_ref, xn_ref = rest[-2:]
    f = pl.program_id(1)

    @pl.when(f == 0)
    def _():
        h = h_ref[...]
        xn_ref[...] = _rms(h, g_ref[...]).astype(BF16)
        o_ref[...] = h

    xn = xn_ref[...]
    gate = _dot(xn, wg_ref[...].astype(BF16))
    up = _dot(xn, wu_ref[...].astype(BF16))
    act = (gate * jax.nn.sigmoid(gate) * (0.5 * up)).astype(BF16)
    o_ref[...] += _dot(act, wd_ref[...].astype(BF16))

    if final:
        @pl.when(f == nf - 1)
        def _():
            o_ref[...] = _rms(o_ref[...], fg_ref[...])


def _ffn(h, ln, wgu, wd, layer, *, final_gain=None, tm=1024, tf=256):
    S, D = h.shape
    F = wd.shape[1]
    nf = F // tf
    final = final_gain is not None
    in_specs = [
        pl.BlockSpec((tm, D), lambda i, f: (i, 0)),
        pl.BlockSpec((None, 1, D), lambda i, f: (layer, 0, 0)),
        pl.BlockSpec((None, D, tf), lambda i, f: (layer, 0, f)),
        pl.BlockSpec((None, D, tf), lambda i, f: (layer, 0, f + nf)),
        pl.BlockSpec((None, tf, D), lambda i, f: (layer, f, 0)),
    ]
    args = [h, ln, wgu, wgu, wd]
    if final:
        in_specs.append(pl.BlockSpec((1, D), lambda i, f: (0, 0)))
        args.append(final_gain)
    return pl.pallas_call(
        functools.partial(_ffn_kernel, nf=nf, final=final),
        grid=(S // tm, nf),
        in_specs=in_specs,
        out_specs=pl.BlockSpec((tm, D), lambda i, f: (i, 0)),
        out_shape=jax.ShapeDtypeStruct((S, D), F32),
        scratch_shapes=[pltpu.VMEM((tm, D), BF16)],
        compiler_params=_params(_ARB2),
        name="ffn_final" if final else "ffn",
    )(*args)


def _qkv_kernel(h_ref, g_ref, w_ref, o_ref, xn_ref, *, n_q_blocks, scale):
    j = pl.program_id(1)

    @pl.when(j == 0)
    def _():
        xn_ref[...] = _rms(h_ref[...], g_ref[...]).astype(BF16)

    acc = _dot(xn_ref[...], w_ref[...].astype(BF16))
    o_ref[...] = (acc * jnp.where(j < n_q_blocks, scale, 1.0)).astype(BF16)


def _moba_qkv(h, ln, w, layer, *, tm=1024, tn=1024):
    S, D = h.shape
    N = w.shape[2]
    dh = D // N_HEADS
    return pl.pallas_call(
        functools.partial(_qkv_kernel, n_q_blocks=D // tn, scale=LOG2E * float(dh) ** -0.5),
        grid=(S // tm, N // tn),
        in_specs=[
            pl.BlockSpec((tm, D), lambda i, j: (i, 0)),
            pl.BlockSpec((None, 1, D), lambda i, j: (layer, 0, 0)),
            pl.BlockSpec((None, D, tn), lambda i, j: (layer, 0, j)),
        ],
        out_specs=pl.BlockSpec((tm, tn), lambda i, j: (i, j)),
        out_shape=jax.ShapeDtypeStruct((S, N), BF16),
        scratch_shapes=[pltpu.VMEM((tm, D), BF16)],
        compiler_params=_params(_ARB2),
        name="moba_qkv",
    )(h, ln, w)


def _tile_update(s_ref, v, m_ref, acc_ref, p_ref, r0, *, bias_ref=None, shift=None, causal=False):
    rows, tk = s_ref.shape[0] - r0, s_ref.shape[1]
    for c0 in range(r0, r0 + rows, ATTN_ROWS):
        c1 = c0 + ATTN_ROWS
        s = s_ref[c0:c1, :]
        if bias_ref is not None:
            s = s + bias_ref[c0 - r0:c1 - r0, :]
        if causal:
            qpos = lax.broadcasted_iota(jnp.int32, s.shape, 0) + (c0 - r0)
            kpos = lax.broadcasted_iota(jnp.int32, s.shape, 1)
            s = jnp.where(kpos <= qpos, s, -jnp.inf)
        m_old = m_ref[c0:c1, :]
        row_max = jnp.max(s, axis=-1, keepdims=True)
        if shift is not None:
            row_max = row_max - shift
        m_new = jnp.maximum(m_old, row_max)
        m_ref[c0:c1, :] = m_new
        sub = m_new if shift is None else m_new + shift
        p_ref[c0:c1, :] = jnp.exp2(s - jnp.tile(sub, (1, tk // LANES))).astype(BF16)
        alpha = jnp.exp2(m_old - m_new)
        acc_ref[c0:c1, :] = jnp.tile(alpha, (1, acc_ref.shape[1] // LANES)) * acc_ref[c0:c1, :]
    acc_ref[r0:, :] += _dot(p_ref[r0:, :], v)


def _causal_attention(qi, score_fn, va_ref, s_refs, m_ref, acc_ref, p_ref, o_ref,
                      *, tq, tk, bias_ref=None, slope2=None):
    per = tq // tk
    assert per % 2 == 0
    dv = o_ref.shape[1]
    m_ref[...] = jnp.full_like(m_ref, M_INIT)
    acc_ref[...] = jnp.zeros_like(acc_ref)
    s_refs[0][...] = score_fn(0, 0)

    def tile(t, rd, wr):
        start = pl.multiple_of(t * tk, tk)
        wr[...] = score_fn(0, start + tk)
        shift = None if slope2 is None else slope2 * (qi * tq - t * tk).astype(F32)
        _tile_update(rd, va_ref[pl.ds(start, tk), :], m_ref, acc_ref, p_ref, 0,
                     bias_ref=bias_ref, shift=shift)

    def pair(j, carry):
        tile(2 * j, s_refs[0], s_refs[1])
        tile(2 * j + 1, s_refs[1], s_refs[0])
        return carry

    lax.fori_loop(0, qi * (per // 2), pair, 0)

    for b in range(per):
        r0 = b * tk
        start = pl.multiple_of(qi * tq + r0, tk)
        rd, wr = s_refs[b % 2], s_refs[1 - b % 2]
        if b + 1 < per:
            wr[r0 + tk:, :] = score_fn(r0 + tk, start + tk)
        _tile_update(rd, va_ref[pl.ds(start, tk), :], m_ref, acc_ref, p_ref, r0,
                     bias_ref=bias_ref, causal=True)

    o_ref[...] = (acc_ref[:, :dv] / acc_ref[:, dv:]).astype(BF16)


def _attn_scratch(S, tq, tk, dv):
    return [
        pltpu.VMEM((S, dv + LANES), BF16),
        pltpu.VMEM((tq, tk), F32),
        pltpu.VMEM((tq, tk), F32),
        pltpu.VMEM((tq, LANES), F32),
        pltpu.VMEM((tq, dv + LANES), F32),
        pltpu.VMEM((tq, tk), BF16),
    ]


def _fill_values(va_ref, v_ref):
    dv = v_ref.shape[1]
    va_ref[:, :dv] = v_ref[...]
    va_ref[:, dv:] = jnp.ones((va_ref.shape[0], va_ref.shape[1] - dv), BF16)


def _moba_kernel(slopes_ref, q_ref, k_ref, v_ref, o_ref,
                 kaug_ref, kmean_ref, bias_ref, qaug_ref, va_ref, s0_ref, s1_ref, m_ref, acc_ref, p_ref,
                 *, nb, bs, topk, tq, tk):
    h = pl.program_id(0)
    qi = pl.program_id(1)
    slope2 = slopes_ref[h] * LOG2E
    dh = q_ref.shape[1]

    @pl.when(qi == 0)
    def _():
        _fill_values(va_ref, v_ref)
        kaug_ref[:, :dh] = k_ref[...]
        lane_b = lax.broadcasted_iota(jnp.int32, (bs, LANES), 1)
        kmean_ref[...] = jnp.zeros_like(kmean_ref)
        for n in range(nb):
            kaug_ref[n * bs:(n + 1) * bs, dh:] = (lane_b == n).astype(BF16)
            kmean_ref[n:n + 1, :] = jnp.mean(
                k_ref[n * bs:(n + 1) * bs, :].astype(F32), axis=0, keepdims=True)
        qpos = lax.broadcasted_iota(jnp.int32, (tq, tk), 0)
        kpos = lax.broadcasted_iota(jnp.int32, (tq, tk), 1)
        bias_ref[...] = -slope2 * (qpos - kpos).astype(F32)

    q = q_ref[...]

    nbp = -(-nb // 8) * 8
    gate = _dot_nt(kmean_ref[...].astype(BF16), q)[:nbp, :]
    blk = lax.broadcasted_iota(jnp.int32, (nbp, tq), 0)
    cur = qi * (tq // bs) + lax.broadcasted_iota(jnp.int32, (nbp, tq), 1) // bs
    valid = blk < cur
    blk_f = blk.astype(F32)
    g = jnp.where(valid, gate, -jnp.inf)
    picked = jnp.zeros((nbp, tq), jnp.bool_)
    for _ in range(topk):
        mx = jnp.max(g, axis=0, keepdims=True)
        first = jnp.min(jnp.where(g == mx, blk_f, float(nbp)), axis=0, keepdims=True)
        hit = blk_f == first
        picked = jnp.logical_or(picked, hit)
        g = jnp.where(hit, -jnp.inf, g)
    attend = jnp.logical_or(jnp.logical_and(picked, valid), blk == cur)
    selbias_t = jnp.concatenate(
        [jnp.where(attend, 0.0, MASK_BIAS), jnp.zeros((LANES - nbp, tq), F32)], axis=0)
    qaug_ref[...] = jnp.concatenate([q, selbias_t.T.astype(BF16)], axis=1)

    def score_fn(r0, start):
        return _dot_nt(qaug_ref[r0:, :], kaug_ref[pl.ds(start, tk), :])

    _causal_attention(qi, score_fn, va_ref, (s0_ref, s1_ref), m_ref, acc_ref, p_ref, o_ref,
                      tq=tq, tk=tk, bias_ref=bias_ref, slope2=slope2)


def _moba_attention(qkv, slopes, *, tq=ATTN_TQ, tk=ATTN_TK):
    S, N = qkv.shape
    H = N_HEADS
    dh = N // (3 * H)
    bs = MOBA_BLOCK
    nb = S // bs
    assert S % tq == 0 and tq % tk == 0 and tk % bs == 0 and nb <= LANES and dh == LANES
    return pl.pallas_call(
        functools.partial(_moba_kernel, nb=nb, bs=bs, topk=min(MOBA_TOPK, nb), tq=tq, tk=tk),
        grid=(H, S // tq),
        in_specs=[
            pl.BlockSpec(memory_space=pltpu.SMEM),
            pl.BlockSpec((tq, dh), lambda h, i: (i, h)),
            pl.BlockSpec((S, dh), lambda h, i: (0, H + h)),
            pl.BlockSpec((S, dh), lambda h, i: (0, 2 * H + h)),
        ],
        out_specs=pl.BlockSpec((tq, dh), lambda h, i: (i, h)),
        out_shape=jax.ShapeDtypeStruct((S, H * dh), BF16),
        scratch_shapes=[
            pltpu.VMEM((S, dh + LANES), BF16),
            pltpu.VMEM((LANES, dh), F32),
            pltpu.VMEM((tq, tk), F32),
            pltpu.VMEM((tq, dh + LANES), BF16),
        ] + _attn_scratch(S, tq, tk, dh),
        compiler_params=_params(_ARB2),
        name="moba_attn",
    )(slopes, qkv, qkv, qkv)


def _oproj_kernel(h_ref, o_ref, w_ref, out_ref):
    out_ref[...] = h_ref[...] + _dot(o_ref[...], w_ref[...])


def _oproj(h, o, w, layer, *, tm=512):
    S, D = h.shape
    K = o.shape[1]
    return pl.pallas_call(
        _oproj_kernel,
        grid=(S // tm,),
        in_specs=[
            pl.BlockSpec((tm, D), lambda i: (i, 0)),
            pl.BlockSpec((tm, K), lambda i: (i, 0)),
            pl.BlockSpec((None, K, D), lambda i: (layer, 0, 0)),
        ],
        out_specs=pl.BlockSpec((tm, D), lambda i: (i, 0)),
        out_shape=jax.ShapeDtypeStruct((S, D), F32),
        compiler_params=_params(_ARB1),
        name="oproj",
    )(h, o, w)


def _rope_pair(pair, cs):
    y = pair * cs
    z = y + pltpu.roll(y, LANES // 2, 1)
    lane = lax.broadcasted_iota(jnp.int32, y.shape, 1)
    return jnp.where(lane < LANES // 2, z, 0.0)


def _mla_kv_kernel(h_ref, g_ref, wd_ref, cg_ref, wu_ref, cs_ref, k_ref, v_ref, *, nh):
    hn = _rms(h_ref[...], g_ref[...]).astype(BF16)
    t = _dot(hn, wd_ref[...])
    c = _rms(t[:, :KV_LORA], cg_ref[...]).astype(BF16)
    k_rope = _rope_pair(t[:, KV_LORA:], cs_ref[...])
    for hd in range(nh):
        a = _dot(c, wu_ref[:, hd * (QK_NOPE + V_DIM):(hd + 1) * (QK_NOPE + V_DIM)])
        k_ref[hd] = jnp.concatenate([a[:, :QK_NOPE], k_rope], axis=1).astype(BF16)
        v_ref[hd] = a[:, QK_NOPE:].astype(BF16)


def _mla_kv(h, g, wd, cg, wu, cs, *, tm=256):
    S, D = h.shape
    H = N_HEADS
    return pl.pallas_call(
        functools.partial(_mla_kv_kernel, nh=H),
        grid=(S // tm,),
        in_specs=[
            pl.BlockSpec((tm, D), lambda i: (i, 0)),
            pl.BlockSpec((1, D), lambda i: (0, 0)),
            pl.BlockSpec(wd.shape, lambda i: (0, 0)),
            pl.BlockSpec((1, KV_LORA), lambda i: (0, 0)),
            pl.BlockSpec(wu.shape, lambda i: (0, 0)),
            pl.BlockSpec((tm, LANES), lambda i: (i, 0)),
        ],
        out_specs=[
            pl.BlockSpec((H, tm, HEAD_PAD), lambda i: (0, i, 0)),
            pl.BlockSpec((H, tm, V_DIM), lambda i: (0, i, 0)),
        ],
        out_shape=[
            jax.ShapeDtypeStruct((H, S, HEAD_PAD), BF16),
            jax.ShapeDtypeStruct((H, S, V_DIM), BF16),
        ],
        compiler_params=_params(_ARB1),
        name="mla_kv",
    )(h, g, wd, cg, wu, cs)


def _mla_q_kernel(h_ref, g_ref, wd_ref, cg_ref, wu_ref, cs_ref, q_ref, *, nh, scale):
    xn = _rms(h_ref[...], g_ref[...]).astype(BF16)
    cq = _rms(_dot(xn, wd_ref[...]), cg_ref[...]).astype(BF16)
    cs = cs_ref[...]
    for hd in range(nh):
        a = _dot(cq, wu_ref[:, hd * HEAD_PAD:(hd + 1) * HEAD_PAD])
        q_rope = _rope_pair(a[:, QK_NOPE:], cs)
        q_ref[hd] = (jnp.concatenate([a[:, :QK_NOPE], q_rope], axis=1) * scale).astype(BF16)


def _mla_q(h, ln, wd, cg, wu, cs, layer, j, *, tm=256):
    S, D = h.shape
    H = N_HEADS
    R = wd.shape[2]
    return pl.pallas_call(
        functools.partial(_mla_q_kernel, nh=H, scale=LOG2E * float(QK_NOPE + QK_ROPE) ** -0.5),
        grid=(S // tm,),
        in_specs=[
            pl.BlockSpec((tm, D), lambda i: (i, 0)),
            pl.BlockSpec((None, 1, D), lambda i: (layer, 0, 0)),
            pl.BlockSpec((None, D, R), lambda i: (j, 0, 0)),
            pl.BlockSpec((None, 1, R), lambda i: (j, 0, 0)),
            pl.BlockSpec((None, R, H * HEAD_PAD), lambda i: (j, 0, 0)),
            pl.BlockSpec((tm, LANES), lambda i: (i, 0)),
        ],
        out_specs=pl.BlockSpec((H, tm, HEAD_PAD), lambda i: (0, i, 0)),
        out_shape=jax.ShapeDtypeStruct((H, S, HEAD_PAD), BF16),
        compiler_params=_params(_ARB1),
        name="mla_q",
    )(h, ln, wd, cg, wu, cs)


def _mla_attn_kernel(q_ref, k_ref, v_ref, o_ref, va_ref, s0_ref, s1_ref, m_ref, acc_ref, p_ref,
                     *, tq, tk):
    qi = pl.program_id(1)

    @pl.when(qi == 0)
    def _():
        _fill_values(va_ref, v_ref)

    def score_fn(r0, start):
        return _dot_nt(q_ref[r0:, :], k_ref[pl.ds(start, tk), :])

    _causal_attention(qi, score_fn, va_ref, (s0_ref, s1_ref), m_ref, acc_ref, p_ref, o_ref,
                      tq=tq, tk=tk)


def _mla_attention(q, k, v, *, tq=ATTN_TQ, tk=ATTN_TK):
    H, S, _ = q.shape
    assert S % tq == 0 and tq % tk == 0
    return pl.pallas_call(
        functools.partial(_mla_attn_kernel, tq=tq, tk=tk),
        grid=(H, S // tq),
        in_specs=[
            pl.BlockSpec((None, tq, HEAD_PAD), lambda h, i: (h, i, 0)),
            pl.BlockSpec((None, S, HEAD_PAD), lambda h, i: (h, 0, 0)),
            pl.BlockSpec((None, S, V_DIM), lambda h, i: (h, 0, 0)),
        ],
        out_specs=pl.BlockSpec((tq, V_DIM), lambda h, i: (i, h)),
        out_shape=jax.ShapeDtypeStruct((S, H * V_DIM), BF16),
        scratch_shapes=_attn_scratch(S, tq, tk, V_DIM),
        compiler_params=_params(_ARB2),
        name="mla_attn",
    )(q, k, v)


def _rotate_half_cols(w):
    w1, w2 = jnp.split(w, 2, axis=-1)
    return jnp.concatenate([-w2, w1], axis=-1)


def _alibi_slopes(n):
    return jnp.asarray(np.array([2.0 ** (-8.0 * (i + 1) / n) for i in range(n)], dtype=np.float32))


def _rope_table(s):
    inv = 1.0 / (ROPE_THETA ** (jnp.arange(0, QK_ROPE, 2, dtype=F32) / QK_ROPE))
    ang = jnp.arange(s, dtype=F32)[:, None] * inv[None, :]
    cos, sin = jnp.cos(ang), jnp.sin(ang)
    return jnp.concatenate([cos, cos, sin, sin], axis=-1)


def kernel(x, ln_ffn1, ffn1_wgu, ffn1_wd, ln_mix, ln_ffn2, ffn2_wgu, ffn2_wd, moba_wqkv, moba_wo, kv_norm, mla_wdkv, ckv_norm, mla_wukv, mla_wdq, cq_norm, mla_wuq, mla_wo, final_norm):
    B, S, D = x.shape
    assert B == 1
    depth = ln_ffn1.shape[0]
    n_a = moba_wqkv.shape[0]
    n_b = mla_wdq.shape[0]
    H = N_HEADS

    ln1 = ln_ffn1.reshape(depth, 1, D)
    lnm = ln_mix.reshape(depth, 1, D)
    ln2 = ln_ffn2.reshape(depth, 1, D)
    wqkv, wo_a = moba_wqkv, moba_wo.astype(BF16)
    wo_b = mla_wo.astype(BF16)
    slopes = _alibi_slopes(H)
    cs = _rope_table(S)

    kr = mla_wdkv[:, KV_LORA:]
    wdkv = jnp.concatenate([mla_wdkv, _rotate_half_cols(kr)], axis=-1).astype(BF16)
    wukv = mla_wukv.astype(BF16)
    wdq = mla_wdq.astype(BF16)
    R = mla_wuq.shape[1]
    wuq4 = mla_wuq.reshape(n_b, R, H, QK_NOPE + QK_ROPE)
    rope_cols = wuq4[..., QK_NOPE:]
    wuq = jnp.concatenate([wuq4, _rotate_half_cols(rope_cols)], axis=-1)
    wuq = wuq.reshape(n_b, R, H * HEAD_PAD).astype(BF16)
    cqn = cq_norm.reshape(n_b, 1, R)

    h = x.reshape(S, D)

    fg = final_norm.reshape(1, D)

    def macaron(h, l, mixer):
        h = _ffn(h, ln1, ffn1_wgu, ffn1_wd, l)
        h = mixer(h)
        return _ffn(h, ln2, ffn2_wgu, ffn2_wd, l, final_gain=fg if l == depth - 1 else None)

    for l in range(n_a):
        def moba(h, l=l):
            qkv = _moba_qkv(h, lnm, wqkv, l)
            o = _moba_attention(qkv, slopes)
            return _oproj(h, o, wo_a, l)
        h = macaron(h, l, moba)

    k_sh, v_sh = _mla_kv(h, kv_norm.reshape(1, D), wdkv, ckv_norm.reshape(1, KV_LORA), wukv, cs)

    for j in range(n_b):
        l = n_a + j

        def mla(h, l=l, j=j):
            q = _mla_q(h, lnm, wdq, cqn, wuq, cs, l, j)
            o = _mla_attention(q, k_sh, v_sh)
            return _oproj(h, o, wo_b, j)
        h = macaron(h, l, mla)

    return h.reshape(B, S, D)
```

```python
import functools

import numpy as np
import jax
import jax.numpy as jnp
from jax import lax
from jax.experimental import pallas as pl
from jax.experimental.pallas import tpu as pltpu

F32 = jnp.float32
BF16 = jnp.bfloat16

EPS = 1e-6
ROPE_THETA = 10000.0

N_HEADS = 16
MOBA_BLOCK = 256
MOBA_TOPK = 3
KV_LORA = 512
QK_NOPE = 128
QK_ROPE = 64
V_DIM = 128

LANES = 128
HEAD_PAD = 2 * LANES
MASK_BIAS = -1e9
M_INIT = -1e30
LOG2E = 1.4426950408889634
VMEM_LIMIT = 56 * 1024 * 1024

ATTN_TQ = 1024
ATTN_TK = 512
ATTN_ROWS = 256

_ARB1 = ("arbitrary",)
_ARB2 = ("arbitrary", "arbitrary")


def _params(sem):
    return pltpu.CompilerParams(dimension_semantics=sem, vmem_limit_bytes=VMEM_LIMIT)


def _rms(x, g):
    ms = jnp.mean(x * x, axis=-1, keepdims=True)
    return x * lax.rsqrt(ms + EPS) * g


def _dot(a, b):
    return jnp.dot(a, b, preferred_element_type=F32)


def _dot_nt(a, b):
    return lax.dot_general(a, b, (((1,), (1,)), ((), ())), preferred_element_type=F32)


def _ffn_kernel(h_ref, g_ref, wg_ref, wu_ref, wd_ref, *rest, nf, final):
    fg_ref = rest[0] if final else None
    o_ref, xn_ref = rest[-2:]
    f = pl.program_id(1)

    @pl.when(f == 0)
    def _():
        h = h_ref[...]
        xn_ref[...] = _rms(h, g_ref[...]).astype(BF16)
        o_ref[...] = h

    xn = xn_ref[...]
    gate = _dot(xn, wg_ref[...].astype(BF16))
    up = _dot(xn, wu_ref[...].astype(BF16))
    act = (gate * jax.nn.sigmoid(gate) * (0.5 * up)).astype(BF16)
    o_ref[...] += _dot(act, wd_ref[...].astype(BF16))

    if final:
        @pl.when(f == nf - 1)
        def _():
            o_ref[...] = _rms(o_ref[...], fg_ref[...])


def _ffn(h, ln, wgu, wd, layer, *, final_gain=None, tm=1024, tf=256):
    S, D = h.shape
    F = wd.shape[1]
    nf = F // tf
    final = final_gain is not None
    in_specs = [
        pl.BlockSpec((tm, D), lambda i, f: (i, 0)),
        pl.BlockSpec((None, 1, D), lambda i, f: (layer, 0, 0)),
        pl.BlockSpec((None, D, tf), lambda i, f: (layer, 0, f)),
        pl.BlockSpec((None, D, tf), lambda i, f: (layer, 0, f + nf)),
        pl.BlockSpec((None, tf, D), lambda i, f: (layer, f, 0)),
    ]
    args = [h, ln, wgu, wgu, wd]
    if final:
        in_specs.append(pl.BlockSpec((1, D), lambda i, f: (0, 0)))
        args.append(final_gain)
    return pl.pallas_call(
        functools.partial(_ffn_kernel, nf=nf, final=final),
        grid=(S // tm, nf),
        in_specs=in_specs,
        out_specs=pl.BlockSpec((tm, D), lambda i, f: (i, 0)),
        out_shape=jax.ShapeDtypeStruct((S, D), F32),
        scratch_shapes=[pltpu.VMEM((tm, D), BF16)],
        compiler_params=_params(_ARB2),
        name="ffn_final" if final else "ffn",
    )(*args)


def _qkv_kernel(h_ref, g_ref, w_ref, o_ref, xn_ref, *, n_q_blocks, scale):
    j = pl.program_id(1)

    @pl.when(j == 0)
    def _():
        xn_ref[...] = _rms(h_ref[...], g_ref[...]).astype(BF16)

    acc = _dot(xn_ref[...], w_ref[...].astype(BF16))
    o_ref[...] = (acc * jnp.where(j < n_q_blocks, scale, 1.0)).astype(BF16)


def _moba_qkv(h, ln, w, layer, *, tm=1024, tn=1024):
    S, D = h.shape
    N = w.shape[2]
    dh = D // N_HEADS
    return pl.pallas_call(
        functools.partial(_qkv_kernel, n_q_blocks=D // tn, scale=LOG2E * float(dh) ** -0.5),
        grid=(S // tm, N // tn),
        in_specs=[
            pl.BlockSpec((tm, D), lambda i, j: (i, 0)),
            pl.BlockSpec((None, 1, D), lambda i, j: (layer, 0, 0)),
            pl.BlockSpec((None, D, tn), lambda i, j: (layer, 0, j)),
        ],
        out_specs=pl.BlockSpec((tm, tn), lambda i, j: (i, j)),
        out_shape=jax.ShapeDtypeStruct((S, N), BF16),
        scratch_shapes=[pltpu.VMEM((tm, D), BF16)],
        compiler_params=_params(_ARB2),
        name="moba_qkv",
    )(h, ln, w)


def _tile_update(s_ref, v, m_ref, acc_ref, p_ref, r0, *, shift=None, causal=False):
    rows, tk = s_ref.shape[0] - r0, s_ref.shape[1]
    for c0 in range(r0, r0 + rows, ATTN_ROWS):
        c1 = c0 + ATTN_ROWS
        s = s_ref[c0:c1, :]
        if causal:
            qpos = lax.broadcasted_iota(jnp.int32, s.shape, 0) + (c0 - r0)
            kpos = lax.broadcasted_iota(jnp.int32, s.shape, 1)
            s = jnp.where(kpos <= qpos, s, -jnp.inf)
        m_old = m_ref[c0:c1, :]
        row_max = jnp.max(s, axis=-1, keepdims=True)
        if shift is not None:
            row_max = row_max - shift
        m_new = jnp.maximum(m_old, row_max)
        m_ref[c0:c1, :] = m_new
        sub = m_new if shift is None else m_new + shift
        p_ref[c0:c1, :] = jnp.exp2(s - jnp.tile(sub, (1, tk // LANES))).astype(BF16)
        alpha = jnp.exp2(m_old - m_new)
        acc_ref[c0:c1, :] = jnp.tile(alpha, (1, acc_ref.shape[1] // LANES)) * acc_ref[c0:c1, :]
    acc_ref[r0:, :] += _dot(p_ref[r0:, :], v)


def _causal_attention(qi, score_fn, va_ref, s_refs, m_ref, acc_ref, p_ref, o_ref,
                      *, tq, tk, slope2=None):
    per = tq // tk
    assert per % 2 == 0
    dv = o_ref.shape[1]
    m_ref[...] = jnp.full_like(m_ref, M_INIT)
    acc_ref[...] = jnp.zeros_like(acc_ref)
    s_refs[0][...] = score_fn(0, 0)

    def tile(t, rd, wr):
        start = pl.multiple_of(t * tk, tk)
        wr[...] = score_fn(0, start + tk)
        shift = None if slope2 is None else slope2 * (qi * tq - t * tk).astype(F32)
        _tile_update(rd, va_ref[pl.ds(start, tk), :], m_ref, acc_ref, p_ref, 0, shift=shift)

    def pair(j, carry):
        tile(2 * j, s_refs[0], s_refs[1])
        tile(2 * j + 1, s_refs[1], s_refs[0])
        return carry

    lax.fori_loop(0, qi * (per // 2), pair, 0)

    for b in range(per):
        r0 = b * tk
        start = pl.multiple_of(qi * tq + r0, tk)
        rd, wr = s_refs[b % 2], s_refs[1 - b % 2]
        if b + 1 < per:
            wr[r0 + tk:, :] = score_fn(r0 + tk, start + tk)
        shift = None if slope2 is None else slope2 * float(-r0)
        _tile_update(rd, va_ref[pl.ds(start, tk), :], m_ref, acc_ref, p_ref, r0,
                     shift=shift, causal=True)

    o_ref[...] = (acc_ref[:, :dv] / acc_ref[:, dv:]).astype(BF16)


def _attn_scratch(S, tq, tk, dv):
    return [
        pltpu.VMEM((S, dv + LANES), BF16),
        pltpu.VMEM((tq, tk), F32),
        pltpu.VMEM((tq, tk), F32),
        pltpu.VMEM((tq, LANES), F32),
        pltpu.VMEM((tq, dv + LANES), F32),
        pltpu.VMEM((tq, tk), BF16),
    ]


def _fill_values(va_ref, v_ref):
    dv = v_ref.shape[1]
    va_ref[:, :dv] = v_ref[...]
    va_ref[:, dv:] = jnp.ones((va_ref.shape[0], va_ref.shape[1] - dv), BF16)


def _split3(x):
    hi = x.astype(BF16).astype(F32)
    mid = (x - hi).astype(BF16).astype(F32)
    lo = ((x - hi) - mid).astype(BF16).astype(F32)
    return hi, mid, lo


def _moba_kernel(slopes_ref, q_ref, k_ref, v_ref, o_ref,
                 kaug_ref, kmean_ref, qaug_ref, va_ref, s0_ref, s1_ref, m_ref, acc_ref, p_ref,
                 *, nb, bs, topk, tq, tk):
    h = pl.program_id(0)
    qi = pl.program_id(1)
    slope2 = slopes_ref[h] * LOG2E
    dh = q_ref.shape[1]
    nbp = -(-nb // 8) * 8
    assert nbp + 8 <= LANES

    @pl.when(qi == 0)
    def _():
        _fill_values(va_ref, v_ref)
        kaug_ref[:, :dh] = k_ref[...]
        lane_b = lax.broadcasted_iota(jnp.int32, (bs, LANES), 1)
        row_b = lax.broadcasted_iota(jnp.int32, (bs, LANES), 0)
        kmean_ref[...] = jnp.zeros_like(kmean_ref)
        for n in range(nb):
            hi, mid, lo = _split3(slope2 * (row_b + (n * bs) % tk).astype(F32))
            extra = jnp.where(lane_b == n, 1.0, 0.0)
            extra = jnp.where(lane_b == nbp, hi, extra)
            extra = jnp.where(lane_b == nbp + 1, mid, extra)
            extra = jnp.where(lane_b == nbp + 2, lo, extra)
            kaug_ref[n * bs:(n + 1) * bs, dh:] = extra.astype(BF16)
            kmean_ref[n:n + 1, :] = jnp.mean(
                k_ref[n * bs:(n + 1) * bs, :].astype(F32), axis=0, keepdims=True)

    q = q_ref[...]

    gate = _dot_nt(kmean_ref[...].astype(BF16), q)[:nbp, :]
    blk = lax.broadcasted_iota(jnp.int32, (nbp, tq), 0)
    cur = qi * (tq // bs) + lax.broadcasted_iota(jnp.int32, (nbp, tq), 1) // bs
    valid = blk < cur
    blk_f = blk.astype(F32)
    g = jnp.where(valid, gate, -jnp.inf)
    picked = jnp.zeros((nbp, tq), jnp.bool_)
    for _ in range(topk):
        mx = jnp.max(g, axis=0, keepdims=True)
        first = jnp.min(jnp.where(g == mx, blk_f, float(nbp)), axis=0, keepdims=True)
        hit = blk_f == first
        picked = jnp.logical_or(picked, hit)
        g = jnp.where(hit, -jnp.inf, g)
    attend = jnp.logical_or(jnp.logical_and(picked, valid), blk == cur)
    extra_t = jnp.concatenate(
        [jnp.where(attend, 0.0, MASK_BIAS),
         jnp.ones((8, tq), F32),
         jnp.zeros((LANES - nbp - 8, tq), F32)], axis=0)
    qaug_ref[...] = jnp.concatenate([q, extra_t.T.astype(BF16)], axis=1)

    def score_fn(r0, start):
        return _dot_nt(qaug_ref[r0:, :], kaug_ref[pl.ds(start, tk), :])

    _causal_attention(qi, score_fn, va_ref, (s0_ref, s1_ref), m_ref, acc_ref, p_ref, o_ref,
                      tq=tq, tk=tk, slope2=slope2)


def _moba_attention(qkv, slopes, *, tq=ATTN_TQ, tk=ATTN_TK):
    S, N = qkv.shape
    H = N_HEADS
    dh = N // (3 * H)
    bs = MOBA_BLOCK
    nb = S // bs
    assert S % tq == 0 and tq % tk == 0 and tk % bs == 0 and nb <= LANES and dh == LANES
    return pl.pallas_call(
        functools.partial(_moba_kernel, nb=nb, bs=bs, topk=min(MOBA_TOPK, nb), tq=tq, tk=tk),
        grid=(H, S // tq),
        in_specs=[
            pl.BlockSpec(memory_space=pltpu.SMEM),
            pl.BlockSpec((tq, dh), lambda h, i: (i, h)),
            pl.BlockSpec((S, dh), lambda h, i: (0, H + h)),
            pl.BlockSpec((S, dh), lambda h, i: (0, 2 * H + h)),
        ],
        out_specs=pl.BlockSpec((tq, dh), lambda h, i: (i, h)),
        out_shape=jax.ShapeDtypeStruct((S, H * dh), BF16),
        scratch_shapes=[
            pltpu.VMEM((S, dh + LANES), BF16),
            pltpu.VMEM((LANES, dh), F32),
            pltpu.VMEM((tq, dh + LANES), BF16),
        ] + _attn_scratch(S, tq, tk, dh),
        compiler_params=_params(_ARB2),
        name="moba_attn",
    )(slopes, qkv, qkv, qkv)


def _oproj_kernel(h_ref, o_ref, w_ref, out_ref):
    out_ref[...] = h_ref[...] + _dot(o_ref[...], w_ref[...])


def _oproj(h, o, w, layer, *, tm=512):
    S, D = h.shape
    K = o.shape[1]
    return pl.pallas_call(
        _oproj_kernel,
        grid=(S // tm,),
        in_specs=[
            pl.BlockSpec((tm, D), lambda i: (i, 0)),
            pl.BlockSpec((tm, K), lambda i: (i, 0)),
            pl.BlockSpec((None, K, D), lambda i: (layer, 0, 0)),
        ],
        out_specs=pl.BlockSpec((tm, D), lambda i: (i, 0)),
        out_shape=jax.ShapeDtypeStruct((S, D), F32),
        compiler_params=_params(_ARB1),
        name="oproj",
    )(h, o, w)


def _rope_pair(pair, cs):
    y = pair * cs
    z = y + pltpu.roll(y, LANES // 2, 1)
    lane = lax.broadcasted_iota(jnp.int32, y.shape, 1)
    return jnp.where(lane < LANES // 2, z, 0.0)


def _mla_kv_kernel(h_ref, g_ref, wd_ref, cg_ref, wu_ref, cs_ref, k_ref, v_ref, *, nh):
    hn = _rms(h_ref[...], g_ref[...]).astype(BF16)
    t = _dot(hn, wd_ref[...])
    c = _rms(t[:, :KV_LORA], cg_ref[...]).astype(BF16)
    k_rope = _rope_pair(t[:, KV_LORA:], cs_ref[...])
    for hd in range(nh):
        a = _dot(c, wu_ref[:, hd * (QK_NOPE + V_DIM):(hd + 1) * (QK_NOPE + V_DIM)])
        k_ref[hd] = jnp.concatenate([a[:, :QK_NOPE], k_rope], axis=1).astype(BF16)
        v_ref[hd] = a[:, QK_NOPE:].astype(BF16)


def _mla_kv(h, g, wd, cg, wu, cs, *, tm=512):
    S, D = h.shape
    H = N_HEADS
    return pl.pallas_call(
        functools.partial(_mla_kv_kernel, nh=H),
        grid=(S // tm,),
        in_specs=[
            pl.BlockSpec((tm, D), lambda i: (i, 0)),
            pl.BlockSpec((1, D), lambda i: (0, 0)),
            pl.BlockSpec(wd.shape, lambda i: (0, 0)),
            pl.BlockSpec((1, KV_LORA), lambda i: (0, 0)),
            pl.BlockSpec(wu.shape, lambda i: (0, 0)),
            pl.BlockSpec((tm, LANES), lambda i: (i, 0)),
        ],
        out_specs=[
            pl.BlockSpec((H, tm, HEAD_PAD), lambda i: (0, i, 0)),
            pl.BlockSpec((H, tm, V_DIM), lambda i: (0, i, 0)),
        ],
        out_shape=[
            jax.ShapeDtypeStruct((H, S, HEAD_PAD), BF16),
            jax.ShapeDtypeStruct((H, S, V_DIM), BF16),
        ],
        compiler_params=_params(_ARB1),
        name="mla_kv",
    )(h, g, wd, cg, wu, cs)


def _mla_q_kernel(h_ref, g_ref, wd_ref, cg_ref, wu_ref, cs_ref, q_ref, *, nh, scale):
    xn = _rms(h_ref[...], g_ref[...]).astype(BF16)
    cq = _rms(_dot(xn, wd_ref[...]), cg_ref[...]).astype(BF16)
    cs = cs_ref[...]
    for hd in range(nh):
        a = _dot(cq, wu_ref[:, hd * HEAD_PAD:(hd + 1) * HEAD_PAD])
        q_rope = _rope_pair(a[:, QK_NOPE:], cs)
        q_ref[hd] = (jnp.concatenate([a[:, :QK_NOPE], q_rope], axis=1) * scale).astype(BF16)


def _mla_q(h, ln, wd, cg, wu, cs, layer, j, *, tm=512):
    S, D = h.shape
    H = N_HEADS
    R = wd.shape[2]
    return pl.pallas_call(
        functools.partial(_mla_q_kernel, nh=H, scale=LOG2E * float(QK_NOPE + QK_ROPE) ** -0.5),
        grid=(S // tm,),
        in_specs=[
            pl.BlockSpec((tm, D), lambda i: (i, 0)),
            pl.BlockSpec((None, 1, D), lambda i: (layer, 0, 0)),
            pl.BlockSpec((None, D, R), lambda i: (j, 0, 0)),
            pl.BlockSpec((None, 1, R), lambda i: (j, 0, 0)),
            pl.BlockSpec((None, R, H * HEAD_PAD), lambda i: (j, 0, 0)),
            pl.BlockSpec((tm, LANES), lambda i: (i, 0)),
        ],
        out_specs=pl.BlockSpec((H, tm, HEAD_PAD), lambda i: (0, i, 0)),
        out_shape=jax.ShapeDtypeStruct((H, S, HEAD_PAD), BF16),
        compiler_params=_params(_ARB1),
        name="mla_q",
    )(h, ln, wd, cg, wu, cs)


def _mla_attn_kernel(q_ref, k_ref, v_ref, o_ref, va_ref, s0_ref, s1_ref, m_ref, acc_ref, p_ref,
                     *, tq, tk):
    qi = pl.program_id(1)

    @pl.when(qi == 0)
    def _():
        _fill_values(va_ref, v_ref)

    def score_fn(r0, start):
        return _dot_nt(q_ref[r0:, :], k_ref[pl.ds(start, tk), :])

    _causal_attention(qi, score_fn, va_ref, (s0_ref, s1_ref), m_ref, acc_ref, p_ref, o_ref,
                      tq=tq, tk=tk)


def _mla_attention(q, k, v, *, tq=ATTN_TQ, tk=ATTN_TK):
    H, S, _ = q.shape
    assert S % tq == 0 and tq % tk == 0
    return pl.pallas_call(
        functools.partial(_mla_attn_kernel, tq=tq, tk=tk),
        grid=(H, S // tq),
        in_specs=[
            pl.BlockSpec((None, tq, HEAD_PAD), lambda h, i: (h, i, 0)),
            pl.BlockSpec((None, S, HEAD_PAD), lambda h, i: (h, 0, 0)),
            pl.BlockSpec((None, S, V_DIM), lambda h, i: (h, 0, 0)),
        ],
        out_specs=pl.BlockSpec((tq, V_DIM), lambda h, i: (i, h)),
        out_shape=jax.ShapeDtypeStruct((S, H * V_DIM), BF16),
        scratch_shapes=_attn_scratch(S, tq, tk, V_DIM),
        compiler_params=_params(_ARB2),
        name="mla_attn",
    )(q, k, v)


def _rotate_half_cols(w):
    w1, w2 = jnp.split(w, 2, axis=-1)
    return jnp.concatenate([-w2, w1], axis=-1)


def _alibi_slopes(n):
    return jnp.asarray(np.array([2.0 ** (-8.0 * (i + 1) / n) for i in range(n)], dtype=np.float32))


def _rope_table(s):
    inv = 1.0 / (ROPE_THETA ** (jnp.arange(0, QK_ROPE, 2, dtype=F32) / QK_ROPE))
    ang = jnp.arange(s, dtype=F32)[:, None] * inv[None, :]
    cos, sin = jnp.cos(ang), jnp.sin(ang)
    return jnp.concatenate([cos, cos, sin, sin], axis=-1)


def kernel(x, ln_ffn1, ffn1_wgu, ffn1_wd, ln_mix, ln_ffn2, ffn2_wgu, ffn2_wd, moba_wqkv, moba_wo, kv_norm, mla_wdkv, ckv_norm, mla_wukv, mla_wdq, cq_norm, mla_wuq, mla_wo, final_norm):
    B, S, D = x.shape
    assert B == 1
    depth = ln_ffn1.shape[0]
    n_a = moba_wqkv.shape[0]
    n_b = mla_wdq.shape[0]
    H = N_HEADS

    ln1 = ln_ffn1.reshape(depth, 1, D)
    lnm = ln_mix.reshape(depth, 1, D)
    ln2 = ln_ffn2.reshape(depth, 1, D)
    wqkv, wo_a = moba_wqkv, moba_wo.astype(BF16)
    wo_b = mla_wo.astype(BF16)
    slopes = _alibi_slopes(H)
    cs = _rope_table(S)

    kr = mla_wdkv[:, KV_LORA:]
    wdkv = jnp.concatenate([mla_wdkv, _rotate_half_cols(kr)], axis=-1).astype(BF16)
    wukv = mla_wukv.astype(BF16)
    wdq = mla_wdq.astype(BF16)
    R = mla_wuq.shape[1]
    wuq4 = mla_wuq.reshape(n_b, R, H, QK_NOPE + QK_ROPE)
    rope_cols = wuq4[..., QK_NOPE:]
    wuq = jnp.concatenate([wuq4, _rotate_half_cols(rope_cols)], axis=-1)
    wuq = wuq.reshape(n_b, R, H * HEAD_PAD).astype(BF16)
    cqn = cq_norm.reshape(n_b, 1, R)

    h = x.reshape(S, D)

    fg = final_norm.reshape(1, D)

    def macaron(h, l, mixer):
        h = _ffn(h, ln1, ffn1_wgu, ffn1_wd, l)
        h = mixer(h)
        return _ffn(h, ln2, ffn2_wgu, ffn2_wd, l, final_gain=fg if l == depth - 1 else None)

    for l in range(n_a):
        def moba(h, l=l):
            qkv = _moba_qkv(h, lnm, wqkv, l)
            o = _moba_attention(qkv, slopes)
            return _oproj(h, o, wo_a, l)
        h = macaron(h, l, moba)

    k_sh, v_sh = _mla_kv(h, kv_norm.reshape(1, D), wdkv, ckv_norm.reshape(1, KV_LORA), wukv, cs)

    for j in range(n_b):
        l = n_a + j

        def mla(h, l=l, j=j):
            q = _mla_q(h, lnm, wdq, cqn, wuq, cs, l, j)
            o = _mla_attention(q, k_sh, v_sh)
            return _oproj(h, o, wo_b, j)
        h = macaron(h, l, mla)

    return h.reshape(B, S, D)
```

```python
import functools

import numpy as np
import jax
import jax.numpy as jnp
from jax import lax
from jax.experimental import pallas as pl
from jax.experimental.pallas import tpu as pltpu

F32 = jnp.float32
BF16 = jnp.bfloat16

EPS = 1e-6
ROPE_THETA = 10000.0

N_HEADS = 16
MOBA_BLOCK = 256
MOBA_TOPK = 3
KV_LORA = 512
QK_NOPE = 128
QK_ROPE = 64
V_DIM = 128

LANES = 128
HEAD_PAD = 2 * LANES
MASK_BIAS = -1e9
M_INIT = -1e30
LOG2E = 1.4426950408889634
VMEM_LIMIT = 56 * 1024 * 1024
FFN_VMEM_LIMIT = 60 * 1024 * 1024

ATTN_TQ = 1024
ATTN_TK = 512
ATTN_ROWS = 256

_ARB1 = ("arbitrary",)
_ARB2 = ("arbitrary", "arbitrary")


def _params(sem, vmem_limit=VMEM_LIMIT):
    return pltpu.CompilerParams(dimension_semantics=sem, vmem_limit_bytes=vmem_limit)


def _rms(x, g):
    ms = jnp.mean(x * x, axis=-1, keepdims=True)
    return x * lax.rsqrt(ms + EPS) * g


def _dot(a, b):
    return jnp.dot(a, b, preferred_element_type=F32)


def _dot_nt(a, b):
    return lax.dot_general(a, b, (((1,), (1,)), ((), ())), preferred_element_type=F32)


def _ffn_kernel(h_ref, g_ref, wg_ref, wu_ref, wd_ref, *rest, nf, final):
    fg_ref = rest[0] if final else None
    o_ref, xn_ref = rest[-2:]
    f = pl.program_id(1)

    @pl.when(f == 0)
    def _():
        h = h_ref[...]
        xn_ref[...] = _rms(h, g_ref[...]).astype(BF16)
        o_ref[...] = h

    xn = xn_ref[...]
    gate = _dot(xn, wg_ref[...].astype(BF16))
    up = _dot(xn, wu_ref[...].astype(BF16))
    act = (gate * jax.nn.sigmoid(gate) * (0.5 * up)).astype(BF16)
    o_ref[...] += _dot(act, wd_ref[...].astype(BF16))

    if final:
        @pl.when(f == nf - 1)
        def _():
            o_ref[...] = _rms(o_ref[...], fg_ref[...])


def _ffn(h, ln, wgu, wd, layer, *, final_gain=None, tm=1024, tf=512):
    S, D = h.shape
    F = wd.shape[1]
    nf = F // tf
    final = final_gain is not None
    in_specs = [
        pl.BlockSpec((tm, D), lambda i, f: (i, 0), pipeline_mode=pl.Buffered(1)),
        pl.BlockSpec((None, 1, D), lambda i, f: (layer, 0, 0)),
        pl.BlockSpec((None, D, tf), lambda i, f: (layer, 0, f)),
        pl.BlockSpec((None, D, tf), lambda i, f: (layer, 0, f + nf)),
        pl.BlockSpec((None, tf, D), lambda i, f: (layer, f, 0)),
    ]
    args = [h, ln, wgu, wgu, wd]
    if final:
        in_specs.append(pl.BlockSpec((1, D), lambda i, f: (0, 0)))
        args.append(final_gain)
    return pl.pallas_call(
        functools.partial(_ffn_kernel, nf=nf, final=final),
        grid=(S // tm, nf),
        in_specs=in_specs,
        out_specs=pl.BlockSpec((tm, D), lambda i, f: (i, 0)),
        out_shape=jax.ShapeDtypeStruct((S, D), F32),
        scratch_shapes=[pltpu.VMEM((tm, D), BF16)],
        compiler_params=_params(_ARB2, FFN_VMEM_LIMIT),
        name="ffn_final" if final else "ffn",
    )(*args)


def _qkv_kernel(h_ref, g_ref, w_ref, o_ref, xn_ref, *, n_q_blocks, scale):
    j = pl.program_id(1)

    @pl.when(j == 0)
    def _():
        xn_ref[...] = _rms(h_ref[...], g_ref[...]).astype(BF16)

    acc = _dot(xn_ref[...], w_ref[...].astype(BF16))
    o_ref[...] = (acc * jnp.where(j < n_q_blocks, scale, 1.0)).astype(BF16)


def _moba_qkv(h, ln, w, layer, *, tm=1024, tn=1024):
    S, D = h.shape
    N = w.shape[2]
    dh = D // N_HEADS
    return pl.pallas_call(
        functools.partial(_qkv_kernel, n_q_blocks=D // tn, scale=LOG2E * float(dh) ** -0.5),
        grid=(S // tm, N // tn),
        in_specs=[
            pl.BlockSpec((tm, D), lambda i, j: (i, 0)),
            pl.BlockSpec((None, 1, D), lambda i, j: (layer, 0, 0)),
            pl.BlockSpec((None, D, tn), lambda i, j: (layer, 0, j)),
        ],
        out_specs=pl.BlockSpec((tm, tn), lambda i, j: (i, j)),
        out_shape=jax.ShapeDtypeStruct((S, N), BF16),
        scratch_shapes=[pltpu.VMEM((tm, D), BF16)],
        compiler_params=_params(_ARB2),
        name="moba_qkv",
    )(h, ln, w)


def _tile_update(s_ref, v, m_ref, acc_ref, p_ref, r0, *, shift=None, causal=False):
    rows, tk = s_ref.shape[0] - r0, s_ref.shape[1]
    for c0 in range(r0, r0 + rows, ATTN_ROWS):
        c1 = c0 + ATTN_ROWS
        s = s_ref[c0:c1, :]
        if causal:
            qpos = lax.broadcasted_iota(jnp.int32, s.shape, 0) + (c0 - r0)
            kpos = lax.broadcasted_iota(jnp.int32, s.shape, 1)
            s = jnp.where(kpos <= qpos, s, -jnp.inf)
        m_old = m_ref[c0:c1, :]
        row_max = jnp.max(s, axis=-1, keepdims=True)
        if shift is not None:
            row_max = row_max - shift
        m_new = jnp.maximum(m_old, row_max)
        m_ref[c0:c1, :] = m_new
        sub = m_new if shift is None else m_new + shift
        p_ref[c0:c1, :] = jnp.exp2(s - jnp.tile(sub, (1, tk // LANES))).astype(BF16)
        alpha = jnp.exp2(m_old - m_new)
        acc_ref[c0:c1, :] = jnp.tile(alpha, (1, acc_ref.shape[1] // LANES)) * acc_ref[c0:c1, :]
    acc_ref[r0:, :] += _dot(p_ref[r0:, :], v)


def _causal_attention(qi, score_fn, va_ref, s_refs, m_ref, acc_ref, p_ref, o_ref,
                      *, tq, tk, slope2=None):
    per = tq // tk
    assert per % 2 == 0
    dv = o_ref.shape[1]
    m_ref[...] = jnp.full_like(m_ref, M_INIT)
    acc_ref[...] = jnp.zeros_like(acc_ref)
    s_refs[0][...] = score_fn(0, 0)

    def tile(t, rd, wr):
        start = pl.multiple_of(t * tk, tk)
        wr[...] = score_fn(0, start + tk)
        shift = None if slope2 is None else slope2 * (qi * tq - t * tk).astype(F32)
        _tile_update(rd, va_ref[pl.ds(start, tk), :], m_ref, acc_ref, p_ref, 0, shift=shift)

    def pair(j, carry):
        tile(2 * j, s_refs[0], s_refs[1])
        tile(2 * j + 1, s_refs[1], s_refs[0])
        return carry

    lax.fori_loop(0, qi * (per // 2), pair, 0)

    for b in range(per):
        r0 = b * tk
        start = pl.multiple_of(qi * tq + r0, tk)
        rd, wr = s_refs[b % 2], s_refs[1 - b % 2]
        if b + 1 < per:
            wr[r0 + tk:, :] = score_fn(r0 + tk, start + tk)
        shift = None if slope2 is None else slope2 * float(-r0)
        _tile_update(rd, va_ref[pl.ds(start, tk), :], m_ref, acc_ref, p_ref, r0,
                     shift=shift, causal=True)

    o_ref[...] = (acc_ref[:, :dv] / acc_ref[:, dv:]).astype(BF16)


def _attn_scratch(S, tq, tk, dv):
    return [
        pltpu.VMEM((S, dv + LANES), BF16),
        pltpu.VMEM((tq, tk), F32),
        pltpu.VMEM((tq, tk), F32),
        pltpu.VMEM((tq, LANES), F32),
        pltpu.VMEM((tq, dv + LANES), F32),
        pltpu.VMEM((tq, tk), BF16),
    ]


def _fill_values(va_ref, v_ref):
    dv = v_ref.shape[1]
    va_ref[:, :dv] = v_ref[...]
    va_ref[:, dv:] = jnp.ones((va_ref.shape[0], va_ref.shape[1] - dv), BF16)


def _split3(x):
    hi = x.astype(BF16).astype(F32)
    mid = (x - hi).astype(BF16).astype(F32)
    lo = ((x - hi) - mid).astype(BF16).astype(F32)
    return hi, mid, lo


def _moba_kernel(slopes_ref, q_ref, k_ref, v_ref, o_ref,
                 kaug_ref, kmean_ref, qaug_ref, va_ref, s0_ref, s1_ref, m_ref, acc_ref, p_ref,
                 *, nb, bs, topk, tq, tk):
    h = pl.program_id(0)
    qi = pl.program_id(1)
    slope2 = slopes_ref[h] * LOG2E
    dh = q_ref.shape[1]
    nbp = -(-nb // 8) * 8
    assert nbp + 8 <= LANES

    @pl.when(qi == 0)
    def _():
        _fill_values(va_ref, v_ref)
        kaug_ref[:, :dh] = k_ref[...]
        lane_b = lax.broadcasted_iota(jnp.int32, (bs, LANES), 1)
        row_b = lax.broadcasted_iota(jnp.int32, (bs, LANES), 0)
        kmean_ref[...] = jnp.zeros_like(kmean_ref)
        for n in range(nb):
            hi, mid, lo = _split3(slope2 * (row_b + (n * bs) % tk).astype(F32))
            extra = jnp.where(lane_b == n, 1.0, 0.0)
            extra = jnp.where(lane_b == nbp, hi, extra)
            extra = jnp.where(lane_b == nbp + 1, mid, extra)
            extra = jnp.where(lane_b == nbp + 2, lo, extra)
            kaug_ref[n * bs:(n + 1) * bs, dh:] = extra.astype(BF16)
            kmean_ref[n:n + 1, :] = jnp.mean(
                k_ref[n * bs:(n + 1) * bs, :].astype(F32), axis=0, keepdims=True)

    q = q_ref[...]

    gate = _dot_nt(kmean_ref[...].astype(BF16), q)[:nbp, :]
    blk = lax.broadcasted_iota(jnp.int32, (nbp, tq), 0)
    cur = qi * (tq // bs) + lax.broadcasted_iota(jnp.int32, (nbp, tq), 1) // bs
    valid = blk < cur
    blk_f = blk.astype(F32)
    g = jnp.where(valid, gate, -jnp.inf)
    picked = jnp.zeros((nbp, tq), jnp.bool_)
    for _ in range(topk):
        mx = jnp.max(g, axis=0, keepdims=True)
        first = jnp.min(jnp.where(g == mx, blk_f, float(nbp)), axis=0, keepdims=True)
        hit = blk_f == first
        picked = jnp.logical_or(picked, hit)
        g = jnp.where(hit, -jnp.inf, g)
    attend = jnp.logical_or(jnp.logical_and(picked, valid), blk == cur)
    extra_t = jnp.concatenate(
        [jnp.where(attend, 0.0, MASK_BIAS),
         jnp.ones((8, tq), F32),
         jnp.zeros((LANES - nbp - 8, tq), F32)], axis=0)
    qaug_ref[...] = jnp.concatenate([q, extra_t.T.astype(BF16)], axis=1)

    def score_fn(r0, start):
        return _dot_nt(qaug_ref[r0:, :], kaug_ref[pl.ds(start, tk), :])

    _causal_attention(qi, score_fn, va_ref, (s0_ref, s1_ref), m_ref, acc_ref, p_ref, o_ref,
                      tq=tq, tk=tk, slope2=slope2)


def _moba_attention(qkv, slopes, *, tq=ATTN_TQ, tk=ATTN_TK):
    S, N = qkv.shape
    H = N_HEADS
    dh = N // (3 * H)
    bs = MOBA_BLOCK
    nb = S // bs
    assert S % tq == 0 and tq % tk == 0 and tk % bs == 0 and nb <= LANES and dh == LANES
    return pl.pallas_call(
        functools.partial(_moba_kernel, nb=nb, bs=bs, topk=min(MOBA_TOPK, nb), tq=tq, tk=tk),
        grid=(H, S // tq),
        in_specs=[
            pl.BlockSpec(memory_space=pltpu.SMEM),
            pl.BlockSpec((tq, dh), lambda h, i: (i, h)),
            pl.BlockSpec((S, dh), lambda h, i: (0, H + h)),
            pl.BlockSpec((S, dh), lambda h, i: (0, 2 * H + h)),
        ],
        out_specs=pl.BlockSpec((tq, dh), lambda h, i: (i, h)),
        out_shape=jax.ShapeDtypeStruct((S, H * dh), BF16),
        scratch_shapes=[
            pltpu.VMEM((S, dh + LANES), BF16),
            pltpu.VMEM((LANES, dh), F32),
            pltpu.VMEM((tq, dh + LANES), BF16),
        ] + _attn_scratch(S, tq, tk, dh),
        compiler_params=_params(_ARB2),
        name="moba_attn",
    )(slopes, qkv, qkv, qkv)


def _oproj_kernel(h_ref, o_ref, w_ref, out_ref):
    out_ref[...] = h_ref[...] + _dot(o_ref[...], w_ref[...])


def _oproj(h, o, w, layer, *, tm=512):
    S, D = h.shape
    K = o.shape[1]
    return pl.pallas_call(
        _oproj_kernel,
        grid=(S // tm,),
        in_specs=[
            pl.BlockSpec((tm, D), lambda i: (i, 0)),
            pl.BlockSpec((tm, K), lambda i: (i, 0)),
            pl.BlockSpec((None, K, D), lambda i: (layer, 0, 0)),
        ],
        out_specs=pl.BlockSpec((tm, D), lambda i: (i, 0)),
        out_shape=jax.ShapeDtypeStruct((S, D), F32),
        compiler_params=_params(_ARB1),
        name="oproj",
    )(h, o, w)


def _rope_pair(pair, cs):
    y = pair * cs
    z = y + pltpu.roll(y, LANES // 2, 1)
    lane = lax.broadcasted_iota(jnp.int32, y.shape, 1)
    return jnp.where(lane < LANES // 2, z, 0.0)


def _mla_kv_kernel(h_ref, g_ref, wd_ref, cg_ref, wu_ref, cs_ref, k_ref, v_ref, *, nh):
    hn = _rms(h_ref[...], g_ref[...]).astype(BF16)
    t = _dot(hn, wd_ref[...])
    c = _rms(t[:, :KV_LORA], cg_ref[...]).astype(BF16)
    k_rope = _rope_pair(t[:, KV_LORA:], cs_ref[...])
    for hd in range(nh):
        a = _dot(c, wu_ref[:, hd * (QK_NOPE + V_DIM):(hd + 1) * (QK_NOPE + V_DIM)])
        k_ref[hd] = jnp.concatenate([a[:, :QK_NOPE], k_rope], axis=1).astype(BF16)
        v_ref[hd] = a[:, QK_NOPE:].astype(BF16)


def _mla_kv(h, g, wd, cg, wu, cs, *, tm=512):
    S, D = h.shape
    H = N_HEADS
    return pl.pallas_call(
        functools.partial(_mla_kv_kernel, nh=H),
        grid=(S // tm,),
        in_specs=[
            pl.BlockSpec((tm, D), lambda i: (i, 0)),
            pl.BlockSpec((1, D), lambda i: (0, 0)),
            pl.BlockSpec(wd.shape, lambda i: (0, 0)),
            pl.BlockSpec((1, KV_LORA), lambda i: (0, 0)),
            pl.BlockSpec(wu.shape, lambda i: (0, 0)),
            pl.BlockSpec((tm, LANES), lambda i: (i, 0)),
        ],
        out_specs=[
            pl.BlockSpec((H, tm, HEAD_PAD), lambda i: (0, i, 0)),
            pl.BlockSpec((H, tm, V_DIM), lambda i: (0, i, 0)),
        ],
        out_shape=[
            jax.ShapeDtypeStruct((H, S, HEAD_PAD), BF16),
            jax.ShapeDtypeStruct((H, S, V_DIM), BF16),
        ],
        compiler_params=_params(_ARB1),
        name="mla_kv",
    )(h, g, wd, cg, wu, cs)


def _mla_q_kernel(h_ref, g_ref, wd_ref, cg_ref, wu_ref, cs_ref, q_ref, *, nh, scale):
    xn = _rms(h_ref[...], g_ref[...]).astype(BF16)
    cq = _rms(_dot(xn, wd_ref[...]), cg_ref[...]).astype(BF16)
    cs = cs_ref[...]
    for hd in range(nh):
        a = _dot(cq, wu_ref[:, hd * HEAD_PAD:(hd + 1) * HEAD_PAD])
        q_rope = _rope_pair(a[:, QK_NOPE:], cs)
        q_ref[hd] = (jnp.concatenate([a[:, :QK_NOPE], q_rope], axis=1) * scale).astype(BF16)


def _mla_q(h, ln, wd, cg, wu, cs, layer, j, *, tm=512):
    S, D = h.shape
    H = N_HEADS
    R = wd.shape[2]
    return pl.pallas_call(
        functools.partial(_mla_q_kernel, nh=H, scale=LOG2E * float(QK_NOPE + QK_ROPE) ** -0.5),
        grid=(S // tm,),
        in_specs=[
            pl.BlockSpec((tm, D), lambda i: (i, 0)),
            pl.BlockSpec((None, 1, D), lambda i: (layer, 0, 0)),
            pl.BlockSpec((None, D, R), lambda i: (j, 0, 0)),
            pl.BlockSpec((None, 1, R), lambda i: (j, 0, 0)),
            pl.BlockSpec((None, R, H * HEAD_PAD), lambda i: (j, 0, 0)),
            pl.BlockSpec((tm, LANES), lambda i: (i, 0)),
        ],
        out_specs=pl.BlockSpec((H, tm, HEAD_PAD), lambda i: (0, i, 0)),
        out_shape=jax.ShapeDtypeStruct((H, S, HEAD_PAD), BF16),
        compiler_params=_params(_ARB1),
        name="mla_q",
    )(h, ln, wd, cg, wu, cs)


def _mla_attn_kernel(q_ref, k_ref, v_ref, o_ref, va_ref, s0_ref, s1_ref, m_ref, acc_ref, p_ref,
                     *, tq, tk):
    qi = pl.program_id(1)

    @pl.when(qi == 0)
    def _():
        _fill_values(va_ref, v_ref)

    def score_fn(r0, start):
        return _dot_nt(q_ref[r0:, :], k_ref[pl.ds(start, tk), :])

    _causal_attention(qi, score_fn, va_ref, (s0_ref, s1_ref), m_ref, acc_ref, p_ref, o_ref,
                      tq=tq, tk=tk)


def _mla_attention(q, k, v, *, tq=ATTN_TQ, tk=ATTN_TK):
    H, S, _ = q.shape
    assert S % tq == 0 and tq % tk == 0
    return pl.pallas_call(
        functools.partial(_mla_attn_kernel, tq=tq, tk=tk),
        grid=(H, S // tq),
        in_specs=[
            pl.BlockSpec((None, tq, HEAD_PAD), lambda h, i: (h, i, 0)),
            pl.BlockSpec((None, S, HEAD_PAD), lambda h, i: (h, 0, 0)),
            pl.BlockSpec((None, S, V_DIM), lambda h, i: (h, 0, 0)),
        ],
        out_specs=pl.BlockSpec((tq, V_DIM), lambda h, i: (i, h)),
        out_shape=jax.ShapeDtypeStruct((S, H * V_DIM), BF16),
        scratch_shapes=_attn_scratch(S, tq, tk, V_DIM),
        compiler_params=_params(_ARB2),
        name="mla_attn",
    )(q, k, v)


def _rotate_half_cols(w):
    w1, w2 = jnp.split(w, 2, axis=-1)
    return jnp.concatenate([-w2, w1], axis=-1)


def _alibi_slopes(n):
    return jnp.asarray(np.array([2.0 ** (-8.0 * (i + 1) / n) for i in range(n)], dtype=np.float32))


def _rope_table(s):
    inv = 1.0 / (ROPE_THETA ** (jnp.arange(0, QK_ROPE, 2, dtype=F32) / QK_ROPE))
    ang = jnp.arange(s, dtype=F32)[:, None] * inv[None, :]
    cos, sin = jnp.cos(ang), jnp.sin(ang)
    return jnp.concatenate([cos, cos, sin, sin], axis=-1)


def kernel(x, ln_ffn1, ffn1_wgu, ffn1_wd, ln_mix, ln_ffn2, ffn2_wgu, ffn2_wd, moba_wqkv, moba_wo, kv_norm, mla_wdkv, ckv_norm, mla_wukv, mla_wdq, cq_norm, mla_wuq, mla_wo, final_norm):
    B, S, D = x.shape
    assert B == 1
    depth = ln_ffn1.shape[0]
    n_a = moba_wqkv.shape[0]
    n_b = mla_wdq.shape[0]
    H = N_HEADS

    ln1 = ln_ffn1.reshape(depth, 1, D)
    lnm = ln_mix.reshape(depth, 1, D)
    ln2 = ln_ffn2.reshape(depth, 1, D)
    wqkv, wo_a = moba_wqkv, moba_wo.astype(BF16)
    wo_b = mla_wo.astype(BF16)
    slopes = _alibi_slopes(H)
    cs = _rope_table(S)

    kr = mla_wdkv[:, KV_LORA:]
    wdkv = jnp.concatenate([mla_wdkv, _rotate_half_cols(kr)], axis=-1).astype(BF16)
    wukv = mla_wukv.astype(BF16)
    wdq = mla_wdq.astype(BF16)
    R = mla_wuq.shape[1]
    wuq4 = mla_wuq.reshape(n_b, R, H, QK_NOPE + QK_ROPE)
    rope_cols = wuq4[..., QK_NOPE:]
    wuq = jnp.concatenate([wuq4, _rotate_half_cols(rope_cols)], axis=-1)
    wuq = wuq.reshape(n_b, R, H * HEAD_PAD).astype(BF16)
    cqn = cq_norm.reshape(n_b, 1, R)

    h = x.reshape(S, D)

    fg = final_norm.reshape(1, D)

    def macaron(h, l, mixer):
        h = _ffn(h, ln1, ffn1_wgu, ffn1_wd, l)
        h = mixer(h)
        return _ffn(h, ln2, ffn2_wgu, ffn2_wd, l, final_gain=fg if l == depth - 1 else None)

    for l in range(n_a):
        def moba(h, l=l):
            qkv = _moba_qkv(h, lnm, wqkv, l)
            o = _moba_attention(qkv, slopes)
            return _oproj(h, o, wo_a, l)
        h = macaron(h, l, moba)

    k_sh, v_sh = _mla_kv(h, kv_norm.reshape(1, D), wdkv, ckv_norm.reshape(1, KV_LORA), wukv, cs)

    for j in range(n_b):
        l = n_a + j

        def mla(h, l=l, j=j):
            q = _mla_q(h, lnm, wdq, cqn, wuq, cs, l, j)
            o = _mla_attention(q, k_sh, v_sh)
            return _oproj(h, o, wo_b, j)
        h = macaron(h, l, mla)

    return h.reshape(B, S, D)
```

```python
import functools

import numpy as np
import jax
import jax.numpy as jnp
from jax import lax
from jax.experimental import pallas as pl
from jax.experimental.pallas import tpu as pltpu

F32 = jnp.float32
BF16 = jnp.bfloat16

EPS = 1e-6
ROPE_THETA = 10000.0

N_HEADS = 16
MOBA_BLOCK = 256
MOBA_TOPK = 3
KV_LORA = 512
QK_NOPE = 128
QK_ROPE = 64
V_DIM = 128

LANES = 128
HEAD_PAD = 2 * LANES
MASK_BIAS = -1e9
M_INIT = -1e30
LOG2E = 1.4426950408889634
VMEM_LIMIT = 56 * 1024 * 1024

ATTN_TQ = 1024
ATTN_TK = 512
ATTN_ROWS = 256

_ARB1 = ("arbitrary",)
_ARB2 = ("arbitrary", "arbitrary")


def _params(sem):
    return pltpu.CompilerParams(dimension_semantics=sem, vmem_limit_bytes=VMEM_LIMIT)


def _rms(x, g):
    ms = jnp.mean(x * x, axis=-1, keepdims=True)
    return x * lax.rsqrt(ms + EPS) * g


def _dot(a, b):
    return jnp.dot(a, b, preferred_element_type=F32)


def _dot_nt(a, b):
    return lax.dot_general(a, b, (((1,), (1,)), ((), ())), preferred_element_type=F32)


def _ffn_kernel(h_ref, g_ref, wg_ref, wu_ref, wd_ref, *rest, nf, final):
    fg_ref = rest[0] if final else None
    o_ref, xn_ref = rest[-2:]
    f = pl.program_id(1)

    def half_step(xn):
        gate = _dot(xn, wg_ref[...].astype(BF16))
        up = _dot(xn, wu_ref[...].astype(BF16))
        act = (gate * jax.nn.sigmoid(gate) * (0.5 * up)).astype(BF16)
        return _dot(act, wd_ref[...].astype(BF16))

    @pl.when(f == 0)
    def _():
        h = h_ref[...]
        xn = _rms(h, g_ref[...]).astype(BF16)
        xn_ref[...] = xn
        o_ref[...] = h + half_step(xn)

    @pl.when(f > 0)
    def _():
        o_ref[...] += half_step(xn_ref[...])

    if final:
        @pl.when(f == nf - 1)
        def _():
            o_ref[...] = _rms(o_ref[...], fg_ref[...])


def _ffn(h, ln, wgu, wd, layer, *, final_gain=None, tm=1024, tf=256):
    S, D = h.shape
    F = wd.shape[1]
    nf = F // tf
    final = final_gain is not None
    in_specs = [
        pl.BlockSpec((tm, D), lambda i, f: (i, 0)),
        pl.BlockSpec((None, 1, D), lambda i, f: (layer, 0, 0)),
        pl.BlockSpec((None, D, tf), lambda i, f: (layer, 0, f)),
        pl.BlockSpec((None, D, tf), lambda i, f: (layer, 0, f + nf)),
        pl.BlockSpec((None, tf, D), lambda i, f: (layer, f, 0)),
    ]
    args = [h, ln, wgu, wgu, wd]
    if final:
        in_specs.append(pl.BlockSpec((1, D), lambda i, f: (0, 0)))
        args.append(final_gain)
    return pl.pallas_call(
        functools.partial(_ffn_kernel, nf=nf, final=final),
        grid=(S // tm, nf),
        in_specs=in_specs,
        out_specs=pl.BlockSpec((tm, D), lambda i, f: (i, 0)),
        out_shape=jax.ShapeDtypeStruct((S, D), F32),
        scratch_shapes=[pltpu.VMEM((tm, D), BF16)],
        compiler_params=_params(_ARB2),
        name="ffn_final" if final else "ffn",
    )(*args)


def _qkv_kernel(h_ref, g_ref, w_ref, o_ref, xn_ref, *, n_q_blocks, scale):
    j = pl.program_id(1)

    def project(xn):
        acc = _dot(xn, w_ref[...].astype(BF16))
        o_ref[...] = (acc * jnp.where(j < n_q_blocks, scale, 1.0)).astype(BF16)

    @pl.when(j == 0)
    def _():
        xn = _rms(h_ref[...], g_ref[...]).astype(BF16)
        xn_ref[...] = xn
        project(xn)

    @pl.when(j > 0)
    def _():
        project(xn_ref[...])


def _moba_qkv(h, ln, w, layer, *, tm=1024, tn=1024):
    S, D = h.shape
    N = w.shape[2]
    dh = D // N_HEADS
    return pl.pallas_call(
        functools.partial(_qkv_kernel, n_q_blocks=D // tn, scale=LOG2E * float(dh) ** -0.5),
        grid=(S // tm, N // tn),
        in_specs=[
            pl.BlockSpec((tm, D), lambda i, j: (i, 0)),
            pl.BlockSpec((None, 1, D), lambda i, j: (layer, 0, 0)),
            pl.BlockSpec((None, D, tn), lambda i, j: (layer, 0, j)),
        ],
        out_specs=pl.BlockSpec((tm, tn), lambda i, j: (i, j)),
        out_shape=jax.ShapeDtypeStruct((S, N), BF16),
        scratch_shapes=[pltpu.VMEM((tm, D), BF16)],
        compiler_params=_params(_ARB2),
        name="moba_qkv",
    )(h, ln, w)


def _tile_update(s_ref, v, m_ref, acc_ref, p_ref, r0, *, shift=None, causal=False):
    rows, tk = s_ref.shape[0] - r0, s_ref.shape[1]
    for c0 in range(r0, r0 + rows, ATTN_ROWS):
        c1 = c0 + ATTN_ROWS
        s = s_ref[c0:c1, :]
        if causal:
            qpos = lax.broadcasted_iota(jnp.int32, s.shape, 0) + (c0 - r0)
            kpos = lax.broadcasted_iota(jnp.int32, s.shape, 1)
            s = jnp.where(kpos <= qpos, s, -jnp.inf)
        m_old = m_ref[c0:c1, :]
        row_max = jnp.max(s, axis=-1, keepdims=True)
        if shift is not None:
            row_max = row_max - shift
        m_new = jnp.maximum(m_old, row_max)
        m_ref[c0:c1, :] = m_new
        sub = m_new if shift is None else m_new + shift
        p_ref[c0:c1, :] = jnp.exp2(s - jnp.tile(sub, (1, tk // LANES))).astype(BF16)
        alpha = jnp.exp2(m_old - m_new)
        acc_ref[c0:c1, :] = jnp.tile(alpha, (1, acc_ref.shape[1] // LANES)) * acc_ref[c0:c1, :]
    acc_ref[r0:, :] += _dot(p_ref[r0:, :], v)


def _causal_attention(qi, score_fn, va_ref, s_refs, m_ref, acc_ref, p_ref, o_ref,
                      *, tq, tk, slope2=None):
    per = tq // tk
    assert per % 2 == 0
    dv = o_ref.shape[1]
    m_ref[...] = jnp.full_like(m_ref, M_INIT)
    acc_ref[...] = jnp.zeros_like(acc_ref)
    s_refs[0][...] = score_fn(0, 0)

    def tile(t, rd, wr):
        start = pl.multiple_of(t * tk, tk)
        wr[...] = score_fn(0, start + tk)
        shift = None if slope2 is None else slope2 * (qi * tq - t * tk).astype(F32)
        _tile_update(rd, va_ref[pl.ds(start, tk), :], m_ref, acc_ref, p_ref, 0, shift=shift)

    def pair(j, carry):
        tile(2 * j, s_refs[0], s_refs[1])
        tile(2 * j + 1, s_refs[1], s_refs[0])
        return carry

    lax.fori_loop(0, qi * (per // 2), pair, 0)

    for b in range(per):
        r0 = b * tk
        start = pl.multiple_of(qi * tq + r0, tk)
        rd, wr = s_refs[b % 2], s_refs[1 - b % 2]
        if b + 1 < per:
            wr[r0 + tk:, :] = score_fn(r0 + tk, start + tk)
        shift = None if slope2 is None else slope2 * float(-r0)
        _tile_update(rd, va_ref[pl.ds(start, tk), :], m_ref, acc_ref, p_ref, r0,
                     shift=shift, causal=True)

    o_ref[...] = (acc_ref[:, :dv] / acc_ref[:, dv:]).astype(BF16)


def _attn_scratch(S, tq, tk, dv):
    return [
        pltpu.VMEM((S, dv + LANES), BF16),
        pltpu.VMEM((tq, tk), F32),
        pltpu.VMEM((tq, tk), F32),
        pltpu.VMEM((tq, LANES), F32),
        pltpu.VMEM((tq, dv + LANES), F32),
        pltpu.VMEM((tq, tk), BF16),
    ]


def _fill_values(va_ref, v_ref):
    dv = v_ref.shape[1]
    va_ref[:, :dv] = v_ref[...]
    va_ref[:, dv:] = jnp.ones((va_ref.shape[0], va_ref.shape[1] - dv), BF16)


def _split3(x):
    hi = x.astype(BF16).astype(F32)
    mid = (x - hi).astype(BF16).astype(F32)
    lo = ((x - hi) - mid).astype(BF16).astype(F32)
    return hi, mid, lo


def _moba_kernel(slopes_ref, q_ref, k_ref, v_ref, o_ref,
                 kaug_ref, kmean_ref, qaug_ref, va_ref, s0_ref, s1_ref, m_ref, acc_ref, p_ref,
                 *, nb, bs, topk, tq, tk):
    h = pl.program_id(0)
    qi = pl.program_id(1)
    slope2 = slopes_ref[h] * LOG2E
    dh = q_ref.shape[1]
    nbp = -(-nb // 8) * 8
    assert nbp + 8 <= LANES

    @pl.when(qi == 0)
    def _():
        _fill_values(va_ref, v_ref)
        kaug_ref[:, :dh] = k_ref[...]
        lane_b = lax.broadcasted_iota(jnp.int32, (bs, LANES), 1)
        row_b = lax.broadcasted_iota(jnp.int32, (bs, LANES), 0)
        kmean_ref[...] = jnp.zeros_like(kmean_ref)
        for n in range(nb):
            hi, mid, lo = _split3(slope2 * (row_b + (n * bs) % tk).astype(F32))
            extra = jnp.where(lane_b == n, 1.0, 0.0)
            extra = jnp.where(lane_b == nbp, hi, extra)
            extra = jnp.where(lane_b == nbp + 1, mid, extra)
            extra = jnp.where(lane_b == nbp + 2, lo, extra)
            kaug_ref[n * bs:(n + 1) * bs, dh:] = extra.astype(BF16)
            kmean_ref[n:n + 1, :] = jnp.mean(
                k_ref[n * bs:(n + 1) * bs, :].astype(F32), axis=0, keepdims=True)

    q = q_ref[...]

    gate = _dot_nt(kmean_ref[...].astype(BF16), q)[:nbp, :]
    blk = lax.broadcasted_iota(jnp.int32, (nbp, tq), 0)
    cur = qi * (tq // bs) + lax.broadcasted_iota(jnp.int32, (nbp, tq), 1) // bs
    valid = blk < cur
    blk_f = blk.astype(F32)
    g = jnp.where(valid, gate, -jnp.inf)
    picked = jnp.zeros((nbp, tq), jnp.bool_)
    for _ in range(topk):
        mx = jnp.max(g, axis=0, keepdims=True)
        first = jnp.min(jnp.where(g == mx, blk_f, float(nbp)), axis=0, keepdims=True)
        hit = blk_f == first
        picked = jnp.logical_or(picked, hit)
        g = jnp.where(hit, -jnp.inf, g)
    attend = jnp.logical_or(jnp.logical_and(picked, valid), blk == cur)
    extra_t = jnp.concatenate(
        [jnp.where(attend, 0.0, MASK_BIAS),
         jnp.ones((8, tq), F32),
         jnp.zeros((LANES - nbp - 8, tq), F32)], axis=0)
    qaug_ref[...] = jnp.concatenate([q, extra_t.T.astype(BF16)], axis=1)

    def score_fn(r0, start):
        return _dot_nt(qaug_ref[r0:, :], kaug_ref[pl.ds(start, tk), :])

    _causal_attention(qi, score_fn, va_ref, (s0_ref, s1_ref), m_ref, acc_ref, p_ref, o_ref,
                      tq=tq, tk=tk, slope2=slope2)


def _moba_attention(qkv, slopes, *, tq=ATTN_TQ, tk=ATTN_TK):
    S, N = qkv.shape
    H = N_HEADS
    dh = N // (3 * H)
    bs = MOBA_BLOCK
    nb = S // bs
    assert S % tq == 0 and tq % tk == 0 and tk % bs == 0 and nb <= LANES and dh == LANES
    return pl.pallas_call(
        functools.partial(_moba_kernel, nb=nb, bs=bs, topk=min(MOBA_TOPK, nb), tq=tq, tk=tk),
        grid=(H, S // tq),
        in_specs=[
            pl.BlockSpec(memory_space=pltpu.SMEM),
            pl.BlockSpec((tq, dh), lambda h, i: (i, h)),
            pl.BlockSpec((S, dh), lambda h, i: (0, H + h)),
            pl.BlockSpec((S, dh), lambda h, i: (0, 2 * H + h)),
        ],
        out_specs=pl.BlockSpec((tq, dh), lambda h, i: (i, h)),
        out_shape=jax.ShapeDtypeStruct((S, H * dh), BF16),
        scratch_shapes=[
            pltpu.VMEM((S, dh + LANES), BF16),
            pltpu.VMEM((LANES, dh), F32),
            pltpu.VMEM((tq, dh + LANES), BF16),
        ] + _attn_scratch(S, tq, tk, dh),
        compiler_params=_params(_ARB2),
        name="moba_attn",
    )(slopes, qkv, qkv, qkv)


def _oproj_kernel(h_ref, o_ref, w_ref, out_ref):
    out_ref[...] = h_ref[...] + _dot(o_ref[...], w_ref[...])


def _oproj(h, o, w, layer, *, tm=512):
    S, D = h.shape
    K = o.shape[1]
    return pl.pallas_call(
        _oproj_kernel,
        grid=(S // tm,),
        in_specs=[
            pl.BlockSpec((tm, D), lambda i: (i, 0)),
            pl.BlockSpec((tm, K), lambda i: (i, 0)),
            pl.BlockSpec((None, K, D), lambda i: (layer, 0, 0)),
        ],
        out_specs=pl.BlockSpec((tm, D), lambda i: (i, 0)),
        out_shape=jax.ShapeDtypeStruct((S, D), F32),
        compiler_params=_params(_ARB1),
        name="oproj",
    )(h, o, w)


def _rope_pair(pair, cs):
    y = pair * cs
    z = y + pltpu.roll(y, LANES // 2, 1)
    lane = lax.broadcasted_iota(jnp.int32, y.shape, 1)
    return jnp.where(lane < LANES // 2, z, 0.0)


def _mla_kv_kernel(h_ref, g_ref, wd_ref, cg_ref, wu_ref, cs_ref, k_ref, v_ref, *, nh):
    hn = _rms(h_ref[...], g_ref[...]).astype(BF16)
    t = _dot(hn, wd_ref[...])
    c = _rms(t[:, :KV_LORA], cg_ref[...]).astype(BF16)
    k_rope = _rope_pair(t[:, KV_LORA:], cs_ref[...])
    for hd in range(nh):
        a = _dot(c, wu_ref[:, hd * (QK_NOPE + V_DIM):(hd + 1) * (QK_NOPE + V_DIM)])
        k_ref[hd] = jnp.concatenate([a[:, :QK_NOPE], k_rope], axis=1).astype(BF16)
        v_ref[hd] = a[:, QK_NOPE:].astype(BF16)


def _mla_kv(h, g, wd, cg, wu, cs, *, tm=512):
    S, D = h.shape
    H = N_HEADS
    return pl.pallas_call(
        functools.partial(_mla_kv_kernel, nh=H),
        grid=(S // tm,),
        in_specs=[
            pl.BlockSpec((tm, D), lambda i: (i, 0)),
            pl.BlockSpec((1, D), lambda i: (0, 0)),
            pl.BlockSpec(wd.shape, lambda i: (0, 0)),
            pl.BlockSpec((1, KV_LORA), lambda i: (0, 0)),
            pl.BlockSpec(wu.shape, lambda i: (0, 0)),
            pl.BlockSpec((tm, LANES), lambda i: (i, 0)),
        ],
        out_specs=[
            pl.BlockSpec((H, tm, HEAD_PAD), lambda i: (0, i, 0)),
            pl.BlockSpec((H, tm, V_DIM), lambda i: (0, i, 0)),
        ],
        out_shape=[
            jax.ShapeDtypeStruct((H, S, HEAD_PAD), BF16),
            jax.ShapeDtypeStruct((H, S, V_DIM), BF16),
        ],
        compiler_params=_params(_ARB1),
        name="mla_kv",
    )(h, g, wd, cg, wu, cs)


def _mla_q_kernel(h_ref, g_ref, wd_ref, cg_ref, wu_ref, cs_ref, q_ref, *, nh, scale):
    xn = _rms(h_ref[...], g_ref[...]).astype(BF16)
    cq = _rms(_dot(xn, wd_ref[...]), cg_ref[...]).astype(BF16)
    cs = cs_ref[...]
    for hd in range(nh):
        a = _dot(cq, wu_ref[:, hd * HEAD_PAD:(hd + 1) * HEAD_PAD])
        q_rope = _rope_pair(a[:, QK_NOPE:], cs)
        q_ref[hd] = (jnp.concatenate([a[:, :QK_NOPE], q_rope], axis=1) * scale).astype(BF16)


def _mla_q(h, ln, wd, cg, wu, cs, layer, j, *, tm=512):
    S, D = h.shape
    H = N_HEADS
    R = wd.shape[2]
    return pl.pallas_call(
        functools.partial(_mla_q_kernel, nh=H, scale=LOG2E * float(QK_NOPE + QK_ROPE) ** -0.5),
        grid=(S // tm,),
        in_specs=[
            pl.BlockSpec((tm, D), lambda i: (i, 0)),
            pl.BlockSpec((None, 1, D), lambda i: (layer, 0, 0)),
            pl.BlockSpec((None, D, R), lambda i: (j, 0, 0)),
            pl.BlockSpec((None, 1, R), lambda i: (j, 0, 0)),
            pl.BlockSpec((None, R, H * HEAD_PAD), lambda i: (j, 0, 0)),
            pl.BlockSpec((tm, LANES), lambda i: (i, 0)),
        ],
        out_specs=pl.BlockSpec((H, tm, HEAD_PAD), lambda i: (0, i, 0)),
        out_shape=jax.ShapeDtypeStruct((H, S, HEAD_PAD), BF16),
        compiler_params=_params(_ARB1),
        name="mla_q",
    )(h, ln, wd, cg, wu, cs)


def _mla_attn_kernel(q_ref, k_ref, v_ref, o_ref, va_ref, s0_ref, s1_ref, m_ref, acc_ref, p_ref,
                     *, tq, tk):
    qi = pl.program_id(1)

    @pl.when(qi == 0)
    def _():
        _fill_values(va_ref, v_ref)

    def score_fn(r0, start):
        return _dot_nt(q_ref[r0:, :], k_ref[pl.ds(start, tk), :])

    _causal_attention(qi, score_fn, va_ref, (s0_ref, s1_ref), m_ref, acc_ref, p_ref, o_ref,
                      tq=tq, tk=tk)


def _mla_attention(q, k, v, *, tq=ATTN_TQ, tk=ATTN_TK):
    H, S, _ = q.shape
    assert S % tq == 0 and tq % tk == 0
    return pl.pallas_call(
        functools.partial(_mla_attn_kernel, tq=tq, tk=tk),
        grid=(H, S // tq),
        in_specs=[
            pl.BlockSpec((None, tq, HEAD_PAD), lambda h, i: (h, i, 0)),
            pl.BlockSpec((None, S, HEAD_PAD), lambda h, i: (h, 0, 0)),
            pl.BlockSpec((None, S, V_DIM), lambda h, i: (h, 0, 0)),
        ],
        out_specs=pl.BlockSpec((tq, V_DIM), lambda h, i: (i, h)),
        out_shape=jax.ShapeDtypeStruct((S, H * V_DIM), BF16),
        scratch_shapes=_attn_scratch(S, tq, tk, V_DIM),
        compiler_params=_params(_ARB2),
        name="mla_attn",
    )(q, k, v)


def _rotate_half_cols(w):
    w1, w2 = jnp.split(w, 2, axis=-1)
    return jnp.concatenate([-w2, w1], axis=-1)


def _alibi_slopes(n):
    return jnp.asarray(np.array([2.0 ** (-8.0 * (i + 1) / n) for i in range(n)], dtype=np.float32))


def _rope_table(s):
    inv = 1.0 / (ROPE_THETA ** (jnp.arange(0, QK_ROPE, 2, dtype=F32) / QK_ROPE))
    ang = jnp.arange(s, dtype=F32)[:, None] * inv[None, :]
    cos, sin = jnp.cos(ang), jnp.sin(ang)
    return jnp.concatenate([cos, cos, sin, sin], axis=-1)


def kernel(x, ln_ffn1, ffn1_wgu, ffn1_wd, ln_mix, ln_ffn2, ffn2_wgu, ffn2_wd, moba_wqkv, moba_wo, kv_norm, mla_wdkv, ckv_norm, mla_wukv, mla_wdq, cq_norm, mla_wuq, mla_wo, final_norm):
    B, S, D = x.shape
    assert B == 1
    depth = ln_ffn1.shape[0]
    n_a = moba_wqkv.shape[0]
    n_b = mla_wdq.shape[0]
    H = N_HEADS

    ln1 = ln_ffn1.reshape(depth, 1, D)
    lnm = ln_mix.reshape(depth, 1, D)
    ln2 = ln_ffn2.reshape(depth, 1, D)
    wqkv, wo_a = moba_wqkv, moba_wo.astype(BF16)
    wo_b = mla_wo.astype(BF16)
    slopes = _alibi_slopes(H)
    cs = _rope_table(S)

    kr = mla_wdkv[:, KV_LORA:]
    wdkv = jnp.concatenate([mla_wdkv, _rotate_half_cols(kr)], axis=-1).astype(BF16)
    wukv = mla_wukv.astype(BF16)
    wdq = mla_wdq.astype(BF16)
    R = mla_wuq.shape[1]
    wuq4 = mla_wuq.reshape(n_b, R, H, QK_NOPE + QK_ROPE)
    rope_cols = wuq4[..., QK_NOPE:]
    wuq = jnp.concatenate([wuq4, _rotate_half_cols(rope_cols)], axis=-1)
    wuq = wuq.reshape(n_b, R, H * HEAD_PAD).astype(BF16)
    cqn = cq_norm.reshape(n_b, 1, R)

    h = x.reshape(S, D)

    fg = final_norm.reshape(1, D)

    def macaron(h, l, mixer):
        h = _ffn(h, ln1, ffn1_wgu, ffn1_wd, l)
        h = mixer(h)
        return _ffn(h, ln2, ffn2_wgu, ffn2_wd, l, final_gain=fg if l == depth - 1 else None)

    for l in range(n_a):
        def moba(h, l=l):
            qkv = _moba_qkv(h, lnm, wqkv, l)
            o = _moba_attention(qkv, slopes)
            return _oproj(h, o, wo_a, l)
        h = macaron(h, l, moba)

    k_sh, v_sh = _mla_kv(h, kv_norm.reshape(1, D), wdkv, ckv_norm.reshape(1, KV_LORA), wukv, cs)

    for j in range(n_b):
        l = n_a + j

        def mla(h, l=l, j=j):
            q = _mla_q(h, lnm, wdq, cqn, wuq, cs, l, j)
            o = _mla_attention(q, k_sh, v_sh)
            return _oproj(h, o, wo_b, j)
        h = macaron(h, l, mla)

    return h.reshape(B, S, D)
```

```python
import functools

import numpy as np
import jax
import jax.numpy as jnp
from jax import lax
from jax.experimental import pallas as pl
from jax.experimental.pallas import tpu as pltpu

F32 = jnp.float32
BF16 = jnp.bfloat16

EPS = 1e-6
ROPE_THETA = 10000.0

N_HEADS = 16
MOBA_BLOCK = 256
MOBA_TOPK = 3
KV_LORA = 512
QK_NOPE = 128
QK_ROPE = 64
V_DIM = 128

LANES = 128
HEAD_PAD = 2 * LANES
MASK_BIAS = -1e9
M_INIT = -1e30
LOG2E = 1.4426950408889634
VMEM_LIMIT = 56 * 1024 * 1024

ATTN_TQ = 1024
ATTN_TK = 512
ATTN_ROWS = 256

_ARB1 = ("arbitrary",)
_ARB2 = ("arbitrary", "arbitrary")


def _params(sem):
    return pltpu.CompilerParams(dimension_semantics=sem, vmem_limit_bytes=VMEM_LIMIT)


def _rms(x, g):
    ms = jnp.mean(x * x, axis=-1, keepdims=True)
    return x * lax.rsqrt(ms + EPS) * g


def _dot(a, b):
    return jnp.dot(a, b, preferred_element_type=F32)


def _dot_nt(a, b):
    return lax.dot_general(a, b, (((1,), (1,)), ((), ())), preferred_element_type=F32)


def _ffn_kernel(h_ref, g_ref, wg_ref, wu_ref, wd_ref, *rest, nf, final):
    fg_ref = rest[0] if final else None
    o_ref, xn_ref = rest[-2:]
    f = pl.program_id(1)

    def half_step(xn):
        gate = _dot(xn, wg_ref[...].astype(BF16))
        up = _dot(xn, wu_ref[...].astype(BF16))
        act = (gate * jax.nn.sigmoid(gate) * (0.5 * up)).astype(BF16)
        return _dot(act, wd_ref[...].astype(BF16))

    @pl.when(f == 0)
    def _():
        h = h_ref[...]
        xn = _rms(h, g_ref[...]).astype(BF16)
        xn_ref[...] = xn
        o_ref[...] = h + half_step(xn)

    @pl.when(f > 0)
    def _():
        o_ref[...] += half_step(xn_ref[...])

    if final:
        @pl.when(f == nf - 1)
        def _():
            o_ref[...] = _rms(o_ref[...], fg_ref[...])


def _ffn(h, ln, wgu, wd, layer, *, final_gain=None, tm=1024, tf=256):
    S, D = h.shape
    F = wd.shape[1]
    nf = F // tf
    final = final_gain is not None
    in_specs = [
        pl.BlockSpec((tm, D), lambda i, f: (i, 0)),
        pl.BlockSpec((None, 1, D), lambda i, f: (layer, 0, 0)),
        pl.BlockSpec((None, D, tf), lambda i, f: (layer, 0, f)),
        pl.BlockSpec((None, D, tf), lambda i, f: (layer, 0, f + nf)),
        pl.BlockSpec((None, tf, D), lambda i, f: (layer, f, 0)),
    ]
    args = [h, ln, wgu, wgu, wd]
    if final:
        in_specs.append(pl.BlockSpec((1, D), lambda i, f: (0, 0)))
        args.append(final_gain)
    return pl.pallas_call(
        functools.partial(_ffn_kernel, nf=nf, final=final),
        grid=(S // tm, nf),
        in_specs=in_specs,
        out_specs=pl.BlockSpec((tm, D), lambda i, f: (i, 0)),
        out_shape=jax.ShapeDtypeStruct((S, D), F32),
        scratch_shapes=[pltpu.VMEM((tm, D), BF16)],
        compiler_params=_params(_ARB2),
        name="ffn_final" if final else "ffn",
    )(*args)


def _qkv_kernel(h_ref, g_ref, w_ref, o_ref, xn_ref, *, n_q_blocks, scale):
    j = pl.program_id(1)
    heads, _, dh = o_ref.shape

    def project(xn):
        acc = _dot(xn, w_ref[...].astype(BF16))
        acc = (acc * jnp.where(j < n_q_blocks, scale, 1.0)).astype(BF16)
        for hd in range(heads):
            o_ref[hd] = acc[:, hd * dh:(hd + 1) * dh]

    @pl.when(j == 0)
    def _():
        xn = _rms(h_ref[...], g_ref[...]).astype(BF16)
        xn_ref[...] = xn
        project(xn)

    @pl.when(j > 0)
    def _():
        project(xn_ref[...])


def _moba_qkv(h, ln, w, layer, *, tm=1024, tn=1024):
    S, D = h.shape
    N = w.shape[2]
    dh = D // N_HEADS
    return pl.pallas_call(
        functools.partial(_qkv_kernel, n_q_blocks=D // tn, scale=LOG2E * float(dh) ** -0.5),
        grid=(S // tm, N // tn),
        in_specs=[
            pl.BlockSpec((tm, D), lambda i, j: (i, 0)),
            pl.BlockSpec((None, 1, D), lambda i, j: (layer, 0, 0)),
            pl.BlockSpec((None, D, tn), lambda i, j: (layer, 0, j)),
        ],
        out_specs=pl.BlockSpec((tn // dh, tm, dh), lambda i, j: (j, i, 0)),
        out_shape=jax.ShapeDtypeStruct((N // dh, S, dh), BF16),
        scratch_shapes=[pltpu.VMEM((tm, D), BF16)],
        compiler_params=_params(_ARB2),
        name="moba_qkv",
    )(h, ln, w)


def _tile_update(s_ref, v, m_ref, acc_ref, p_ref, r0, *, shift=None, causal=False):
    rows, tk = s_ref.shape[0] - r0, s_ref.shape[1]
    for c0 in range(r0, r0 + rows, ATTN_ROWS):
        c1 = c0 + ATTN_ROWS
        s = s_ref[c0:c1, :]
        if causal:
            qpos = lax.broadcasted_iota(jnp.int32, s.shape, 0) + (c0 - r0)
            kpos = lax.broadcasted_iota(jnp.int32, s.shape, 1)
            s = jnp.where(kpos <= qpos, s, -jnp.inf)
        m_old = m_ref[c0:c1, :]
        row_max = jnp.max(s, axis=-1, keepdims=True)
        if shift is not None:
            row_max = row_max - shift
        m_new = jnp.maximum(m_old, row_max)
        m_ref[c0:c1, :] = m_new
        sub = m_new if shift is None else m_new + shift
        p_ref[c0:c1, :] = jnp.exp2(s - jnp.tile(sub, (1, tk // LANES))).astype(BF16)
        alpha = jnp.exp2(m_old - m_new)
        acc_ref[c0:c1, :] = jnp.tile(alpha, (1, acc_ref.shape[1] // LANES)) * acc_ref[c0:c1, :]
    acc_ref[r0:, :] += _dot(p_ref[r0:, :], v)


def _causal_attention(qi, score_fn, va_ref, s_refs, m_ref, acc_ref, p_ref, o_ref,
                      *, tq, tk, slope2=None):
    per = tq // tk
    assert per % 2 == 0
    dv = o_ref.shape[1]
    m_ref[...] = jnp.full_like(m_ref, M_INIT)
    acc_ref[...] = jnp.zeros_like(acc_ref)
    s_refs[0][...] = score_fn(0, 0)

    def tile(t, rd, wr):
        start = pl.multiple_of(t * tk, tk)
        wr[...] = score_fn(0, start + tk)
        shift = None if slope2 is None else slope2 * (qi * tq - t * tk).astype(F32)
        _tile_update(rd, va_ref[pl.ds(start, tk), :], m_ref, acc_ref, p_ref, 0, shift=shift)

    def pair(j, carry):
        tile(2 * j, s_refs[0], s_refs[1])
        tile(2 * j + 1, s_refs[1], s_refs[0])
        return carry

    lax.fori_loop(0, qi * (per // 2), pair, 0)

    for b in range(per):
        r0 = b * tk
        start = pl.multiple_of(qi * tq + r0, tk)
        rd, wr = s_refs[b % 2], s_refs[1 - b % 2]
        if b + 1 < per:
            wr[r0 + tk:, :] = score_fn(r0 + tk, start + tk)
        shift = None if slope2 is None else slope2 * float(-r0)
        _tile_update(rd, va_ref[pl.ds(start, tk), :], m_ref, acc_ref, p_ref, r0,
                     shift=shift, causal=True)

    o_ref[...] = (acc_ref[:, :dv] / acc_ref[:, dv:]).astype(BF16)


def _attn_scratch(S, tq, tk, dv):
    return [
        pltpu.VMEM((S, dv + LANES), BF16),
        pltpu.VMEM((tq, tk), F32),
        pltpu.VMEM((tq, tk), F32),
        pltpu.VMEM((tq, LANES), F32),
        pltpu.VMEM((tq, dv + LANES), F32),
        pltpu.VMEM((tq, tk), BF16),
    ]


def _fill_values(va_ref, v_ref):
    dv = v_ref.shape[1]
    va_ref[:, :dv] = v_ref[...]
    va_ref[:, dv:] = jnp.ones((va_ref.shape[0], va_ref.shape[1] - dv), BF16)


def _split3(x):
    hi = x.astype(BF16).astype(F32)
    mid = (x - hi).astype(BF16).astype(F32)
    lo = ((x - hi) - mid).astype(BF16).astype(F32)
    return hi, mid, lo


def _moba_kernel(slopes_ref, q_ref, k_ref, v_ref, o_ref,
                 kaug_ref, kmean_ref, qaug_ref, va_ref, s0_ref, s1_ref, m_ref, acc_ref, p_ref,
                 *, nb, bs, topk, tq, tk):
    h = pl.program_id(0)
    qi = pl.program_id(1)
    slope2 = slopes_ref[h] * LOG2E
    dh = q_ref.shape[1]
    nbp = -(-nb // 8) * 8
    assert nbp + 8 <= LANES

    @pl.when(qi == 0)
    def _():
        _fill_values(va_ref, v_ref)
        kaug_ref[:, :dh] = k_ref[...]
        lane_b = lax.broadcasted_iota(jnp.int32, (bs, LANES), 1)
        row_b = lax.broadcasted_iota(jnp.int32, (bs, LANES), 0)
        kmean_ref[...] = jnp.zeros_like(kmean_ref)
        for n in range(nb):
            hi, mid, lo = _split3(slope2 * (row_b + (n * bs) % tk).astype(F32))
            extra = jnp.where(lane_b == n, 1.0, 0.0)
            extra = jnp.where(lane_b == nbp, hi, extra)
            extra = jnp.where(lane_b == nbp + 1, mid, extra)
            extra = jnp.where(lane_b == nbp + 2, lo, extra)
            kaug_ref[n * bs:(n + 1) * bs, dh:] = extra.astype(BF16)
            kmean_ref[n:n + 1, :] = jnp.mean(
                k_ref[n * bs:(n + 1) * bs, :].astype(F32), axis=0, keepdims=True)

    q = q_ref[...]

    gate = _dot_nt(kmean_ref[...].astype(BF16), q)[:nbp, :]
    blk = lax.broadcasted_iota(jnp.int32, (nbp, tq), 0)
    cur = qi * (tq // bs) + lax.broadcasted_iota(jnp.int32, (nbp, tq), 1) // bs
    valid = blk < cur
    blk_f = blk.astype(F32)
    g = jnp.where(valid, gate, -jnp.inf)
    picked = jnp.zeros((nbp, tq), jnp.bool_)
    for _ in range(topk):
        mx = jnp.max(g, axis=0, keepdims=True)
        first = jnp.min(jnp.where(g == mx, blk_f, float(nbp)), axis=0, keepdims=True)
        hit = blk_f == first
        picked = jnp.logical_or(picked, hit)
        g = jnp.where(hit, -jnp.inf, g)
    attend = jnp.logical_or(jnp.logical_and(picked, valid), blk == cur)
    extra_t = jnp.concatenate(
        [jnp.where(attend, 0.0, MASK_BIAS),
         jnp.ones((8, tq), F32),
         jnp.zeros((LANES - nbp - 8, tq), F32)], axis=0)
    qaug_ref[...] = jnp.concatenate([q, extra_t.T.astype(BF16)], axis=1)

    def score_fn(r0, start):
        return _dot_nt(qaug_ref[r0:, :], kaug_ref[pl.ds(start, tk), :])

    _causal_attention(qi, score_fn, va_ref, (s0_ref, s1_ref), m_ref, acc_ref, p_ref, o_ref,
                      tq=tq, tk=tk, slope2=slope2)


def _moba_attention(qkv, slopes, *, tq=ATTN_TQ, tk=ATTN_TK):
    H = N_HEADS
    _, S, dh = qkv.shape
    bs = MOBA_BLOCK
    nb = S // bs
    assert S % tq == 0 and tq % tk == 0 and tk % bs == 0 and nb <= LANES and dh == LANES
    return pl.pallas_call(
        functools.partial(_moba_kernel, nb=nb, bs=bs, topk=min(MOBA_TOPK, nb), tq=tq, tk=tk),
        grid=(H, S // tq),
        in_specs=[
            pl.BlockSpec(memory_space=pltpu.SMEM),
            pl.BlockSpec((None, tq, dh), lambda h, i: (h, i, 0)),
            pl.BlockSpec((None, S, dh), lambda h, i: (H + h, 0, 0)),
            pl.BlockSpec((None, S, dh), lambda h, i: (2 * H + h, 0, 0)),
        ],
        out_specs=pl.BlockSpec((tq, dh), lambda h, i: (i, h)),
        out_shape=jax.ShapeDtypeStruct((S, H * dh), BF16),
        scratch_shapes=[
            pltpu.VMEM((S, dh + LANES), BF16),
            pltpu.VMEM((LANES, dh), F32),
            pltpu.VMEM((tq, dh + LANES), BF16),
        ] + _attn_scratch(S, tq, tk, dh),
        compiler_params=_params(_ARB2),
        name="moba_attn",
    )(slopes, qkv, qkv, qkv)


def _oproj_kernel(h_ref, o_ref, w_ref, out_ref):
    out_ref[...] = h_ref[...] + _dot(o_ref[...], w_ref[...])


def _oproj(h, o, w, layer, *, tm=512):
    S, D = h.shape
    K = o.shape[1]
    return pl.pallas_call(
        _oproj_kernel,
        grid=(S // tm,),
        in_specs=[
            pl.BlockSpec((tm, D), lambda i: (i, 0)),
            pl.BlockSpec((tm, K), lambda i: (i, 0)),
            pl.BlockSpec((None, K, D), lambda i: (layer, 0, 0)),
        ],
        out_specs=pl.BlockSpec((tm, D), lambda i: (i, 0)),
        out_shape=jax.ShapeDtypeStruct((S, D), F32),
        compiler_params=_params(_ARB1),
        name="oproj",
    )(h, o, w)


def _rope_pair(pair, cs):
    y = pair * cs
    z = y + pltpu.roll(y, LANES // 2, 1)
    lane = lax.broadcasted_iota(jnp.int32, y.shape, 1)
    return jnp.where(lane < LANES // 2, z, 0.0)


def _mla_kv_kernel(h_ref, g_ref, wd_ref, cg_ref, wu_ref, cs_ref, k_ref, v_ref, *, nh):
    hn = _rms(h_ref[...], g_ref[...]).astype(BF16)
    t = _dot(hn, wd_ref[...])
    c = _rms(t[:, :KV_LORA], cg_ref[...]).astype(BF16)
    k_rope = _rope_pair(t[:, KV_LORA:], cs_ref[...])
    for hd in range(nh):
        a = _dot(c, wu_ref[:, hd * (QK_NOPE + V_DIM):(hd + 1) * (QK_NOPE + V_DIM)])
        k_ref[hd] = jnp.concatenate([a[:, :QK_NOPE], k_rope], axis=1).astype(BF16)
        v_ref[hd] = a[:, QK_NOPE:].astype(BF16)


def _mla_kv(h, g, wd, cg, wu, cs, *, tm=512):
    S, D = h.shape
    H = N_HEADS
    return pl.pallas_call(
        functools.partial(_mla_kv_kernel, nh=H),
        grid=(S // tm,),
        in_specs=[
            pl.BlockSpec((tm, D), lambda i: (i, 0)),
            pl.BlockSpec((1, D), lambda i: (0, 0)),
            pl.BlockSpec(wd.shape, lambda i: (0, 0)),
            pl.BlockSpec((1, KV_LORA), lambda i: (0, 0)),
            pl.BlockSpec(wu.shape, lambda i: (0, 0)),
            pl.BlockSpec((tm, LANES), lambda i: (i, 0)),
        ],
        out_specs=[
            pl.BlockSpec((H, tm, HEAD_PAD), lambda i: (0, i, 0)),
            pl.BlockSpec((H, tm, V_DIM), lambda i: (0, i, 0)),
        ],
        out_shape=[
            jax.ShapeDtypeStruct((H, S, HEAD_PAD), BF16),
            jax.ShapeDtypeStruct((H, S, V_DIM), BF16),
        ],
        compiler_params=_params(_ARB1),
        name="mla_kv",
    )(h, g, wd, cg, wu, cs)


def _mla_q_kernel(h_ref, g_ref, wd_ref, cg_ref, wu_ref, cs_ref, q_ref, *, nh, scale):
    xn = _rms(h_ref[...], g_ref[...]).astype(BF16)
    cq = _rms(_dot(xn, wd_ref[...]), cg_ref[...]).astype(BF16)
    cs = cs_ref[...]
    for hd in range(nh):
        a = _dot(cq, wu_ref[:, hd * HEAD_PAD:(hd + 1) * HEAD_PAD])
        q_rope = _rope_pair(a[:, QK_NOPE:], cs)
        q_ref[hd] = (jnp.concatenate([a[:, :QK_NOPE], q_rope], axis=1) * scale).astype(BF16)


def _mla_q(h, ln, wd, cg, wu, cs, layer, j, *, tm=512):
    S, D = h.shape
    H = N_HEADS
    R = wd.shape[2]
    return pl.pallas_call(
        functools.partial(_mla_q_kernel, nh=H, scale=LOG2E * float(QK_NOPE + QK_ROPE) ** -0.5),
        grid=(S // tm,),
        in_specs=[
            pl.BlockSpec((tm, D), lambda i: (i, 0)),
            pl.BlockSpec((None, 1, D), lambda i: (layer, 0, 0)),
            pl.BlockSpec((None, D, R), lambda i: (j, 0, 0)),
            pl.BlockSpec((None, 1, R), lambda i: (j, 0, 0)),
            pl.BlockSpec((None, R, H * HEAD_PAD), lambda i: (j, 0, 0)),
            pl.BlockSpec((tm, LANES), lambda i: (i, 0)),
        ],
        out_specs=pl.BlockSpec((H, tm, HEAD_PAD), lambda i: (0, i, 0)),
        out_shape=jax.ShapeDtypeStruct((H, S, HEAD_PAD), BF16),
        compiler_params=_params(_ARB1),
        name="mla_q",
    )(h, ln, wd, cg, wu, cs)


def _mla_attn_kernel(q_ref, k_ref, v_ref, o_ref, va_ref, s0_ref, s1_ref, m_ref, acc_ref, p_ref,
                     *, tq, tk):
    qi = pl.program_id(1)

    @pl.when(qi == 0)
    def _():
        _fill_values(va_ref, v_ref)

    def score_fn(r0, start):
        return _dot_nt(q_ref[r0:, :], k_ref[pl.ds(start, tk), :])

    _causal_attention(qi, score_fn, va_ref, (s0_ref, s1_ref), m_ref, acc_ref, p_ref, o_ref,
                      tq=tq, tk=tk)


def _mla_attention(q, k, v, *, tq=ATTN_TQ, tk=ATTN_TK):
    H, S, _ = q.shape
    assert S % tq == 0 and tq % tk == 0
    return pl.pallas_call(
        functools.partial(_mla_attn_kernel, tq=tq, tk=tk),
        grid=(H, S // tq),
        in_specs=[
            pl.BlockSpec((None, tq, HEAD_PAD), lambda h, i: (h, i, 0)),
            pl.BlockSpec((None, S, HEAD_PAD), lambda h, i: (h, 0, 0)),
            pl.BlockSpec((None, S, V_DIM), lambda h, i: (h, 0, 0)),
        ],
        out_specs=pl.BlockSpec((tq, V_DIM), lambda h, i: (i, h)),
        out_shape=jax.ShapeDtypeStruct((S, H * V_DIM), BF16),
        scratch_shapes=_attn_scratch(S, tq, tk, V_DIM),
        compiler_params=_params(_ARB2),
        name="mla_attn",
    )(q, k, v)


def _rotate_half_cols(w):
    w1, w2 = jnp.split(w, 2, axis=-1)
    return jnp.concatenate([-w2, w1], axis=-1)


def _alibi_slopes(n):
    return jnp.asarray(np.array([2.0 ** (-8.0 * (i + 1) / n) for i in range(n)], dtype=np.float32))


def _rope_table(s):
    inv = 1.0 / (ROPE_THETA ** (jnp.arange(0, QK_ROPE, 2, dtype=F32) / QK_ROPE))
    ang = jnp.arange(s, dtype=F32)[:, None] * inv[None, :]
    cos, sin = jnp.cos(ang), jnp.sin(ang)
    return jnp.concatenate([cos, cos, sin, sin], axis=-1)


def kernel(x, ln_ffn1, ffn1_wgu, ffn1_wd, ln_mix, ln_ffn2, ffn2_wgu, ffn2_wd, moba_wqkv, moba_wo, kv_norm, mla_wdkv, ckv_norm, mla_wukv, mla_wdq, cq_norm, mla_wuq, mla_wo, final_norm):
    B, S, D = x.shape
    assert B == 1
    depth = ln_ffn1.shape[0]
    n_a = moba_wqkv.shape[0]
    n_b = mla_wdq.shape[0]
    H = N_HEADS

    ln1 = ln_ffn1.reshape(depth, 1, D)
    lnm = ln_mix.reshape(depth, 1, D)
    ln2 = ln_ffn2.reshape(depth, 1, D)
    wqkv, wo_a = moba_wqkv, moba_wo.astype(BF16)
    wo_b = mla_wo.astype(BF16)
    slopes = _alibi_slopes(H)
    cs = _rope_table(S)

    kr = mla_wdkv[:, KV_LORA:]
    wdkv = jnp.concatenate([mla_wdkv, _rotate_half_cols(kr)], axis=-1).astype(BF16)
    wukv = mla_wukv.astype(BF16)
    wdq = mla_wdq.astype(BF16)
    R = mla_wuq.shape[1]
    wuq4 = mla_wuq.reshape(n_b, R, H, QK_NOPE + QK_ROPE)
    rope_cols = wuq4[..., QK_NOPE:]
    wuq = jnp.concatenate([wuq4, _rotate_half_cols(rope_cols)], axis=-1)
    wuq = wuq.reshape(n_b, R, H * HEAD_PAD).astype(BF16)
    cqn = cq_norm.reshape(n_b, 1, R)

    h = x.reshape(S, D)

    fg = final_norm.reshape(1, D)

    def macaron(h, l, mixer):
        h = _ffn(h, ln1, ffn1_wgu, ffn1_wd, l)
        h = mixer(h)
        return _ffn(h, ln2, ffn2_wgu, ffn2_wd, l, final_gain=fg if l == depth - 1 else None)

    for l in range(n_a):
        def moba(h, l=l):
            qkv = _moba_qkv(h, lnm, wqkv, l)
            o = _moba_attention(qkv, slopes)
            return _oproj(h, o, wo_a, l)
        h = macaron(h, l, moba)

    k_sh, v_sh = _mla_kv(h, kv_norm.reshape(1, D), wdkv, ckv_norm.reshape(1, KV_LORA), wukv, cs)

    for j in range(n_b):
        l = n_a + j

        def mla(h, l=l, j=j):
            q = _mla_q(h, lnm, wdq, cqn, wuq, cs, l, j)
            o = _mla_attention(q, k_sh, v_sh)
            return _oproj(h, o, wo_b, j)
        h = macaron(h, l, mla)

    return h.reshape(B, S, D)
```

```python
import functools

import numpy as np
import jax
import jax.numpy as jnp
from jax import lax
from jax.experimental import pallas as pl
from jax.experimental.pallas import tpu as pltpu

F32 = jnp.float32
BF16 = jnp.bfloat16

EPS = 1e-6
ROPE_THETA = 10000.0

N_HEADS = 16
MOBA_BLOCK = 256
MOBA_TOPK = 3
KV_LORA = 512
QK_NOPE = 128
QK_ROPE = 64
V_DIM = 128

LANES = 128
HEAD_PAD = 2 * LANES
MASK_BIAS = -1e9
M_INIT = -1e30
LOG2E = 1.4426950408889634
VMEM_LIMIT = 56 * 1024 * 1024

ATTN_TQ = 1024
ATTN_TK = 512
ATTN_ROWS = 256

_ARB1 = ("arbitrary",)
_ARB2 = ("arbitrary", "arbitrary")


def _params(sem):
    return pltpu.CompilerParams(dimension_semantics=sem, vmem_limit_bytes=VMEM_LIMIT)


def _rms(x, g):
    ms = jnp.mean(x * x, axis=-1, keepdims=True)
    return x * lax.rsqrt(ms + EPS) * g


def _dot(a, b):
    return jnp.dot(a, b, preferred_element_type=F32)


def _dot_nt(a, b):
    return lax.dot_general(a, b, (((1,), (1,)), ((), ())), preferred_element_type=F32)


def _ffn_kernel(h_ref, g_ref, wg_ref, wu_ref, wd_ref, *rest, nf, final):
    fg_ref = rest[0] if final else None
    o_ref, xn_ref = rest[-2:]
    f = pl.program_id(1)

    def half_step(xn):
        gate = _dot(xn, wg_ref[...].astype(BF16))
        up = _dot(xn, wu_ref[...].astype(BF16))
        act = (gate * jax.nn.sigmoid(gate) * (0.5 * up)).astype(BF16)
        return _dot(act, wd_ref[...].astype(BF16))

    @pl.when(f == 0)
    def _():
        h = h_ref[...]
        xn = _rms(h, g_ref[...]).astype(BF16)
        xn_ref[...] = xn
        o_ref[...] = h + half_step(xn)

    @pl.when(f > 0)
    def _():
        o_ref[...] += half_step(xn_ref[...])

    if final:
        @pl.when(f == nf - 1)
        def _():
            o_ref[...] = _rms(o_ref[...], fg_ref[...])


def _ffn(h, ln, wgu, wd, layer, *, final_gain=None, tm=1024, tf=256):
    S, D = h.shape
    F = wd.shape[1]
    nf = F // tf
    final = final_gain is not None
    in_specs = [
        pl.BlockSpec((tm, D), lambda i, f: (i, 0)),
        pl.BlockSpec((None, 1, D), lambda i, f: (layer, 0, 0)),
        pl.BlockSpec((None, D, tf), lambda i, f: (layer, 0, f)),
        pl.BlockSpec((None, D, tf), lambda i, f: (layer, 0, f + nf)),
        pl.BlockSpec((None, tf, D), lambda i, f: (layer, f, 0)),
    ]
    args = [h, ln, wgu, wgu, wd]
    if final:
        in_specs.append(pl.BlockSpec((1, D), lambda i, f: (0, 0)))
        args.append(final_gain)
    return pl.pallas_call(
        functools.partial(_ffn_kernel, nf=nf, final=final),
        grid=(S // tm, nf),
        in_specs=in_specs,
        out_specs=pl.BlockSpec((tm, D), lambda i, f: (i, 0)),
        out_shape=jax.ShapeDtypeStruct((S, D), F32),
        scratch_shapes=[pltpu.VMEM((tm, D), BF16)],
        compiler_params=_params(_ARB2),
        name="ffn_final" if final else "ffn",
    )(*args)


def _qkv_kernel(h_ref, g_ref, w_ref, o_ref, xn_ref, *, n_q_blocks, scale):
    j = pl.program_id(1)
    heads, _, dh = o_ref.shape

    def project(xn):
        acc = _dot(xn, w_ref[...].astype(BF16))
        acc = (acc * jnp.where(j < n_q_blocks, scale, 1.0)).astype(BF16)
        for hd in range(heads):
            o_ref[hd] = acc[:, hd * dh:(hd + 1) * dh]

    @pl.when(j == 0)
    def _():
        xn = _rms(h_ref[...], g_ref[...]).astype(BF16)
        xn_ref[...] = xn
        project(xn)

    @pl.when(j > 0)
    def _():
        project(xn_ref[...])


def _moba_qkv(h, ln, w, layer, *, tm=1024, tn=1024):
    S, D = h.shape
    N = w.shape[2]
    dh = D // N_HEADS
    return pl.pallas_call(
        functools.partial(_qkv_kernel, n_q_blocks=D // tn, scale=LOG2E * float(dh) ** -0.5),
        grid=(S // tm, N // tn),
        in_specs=[
            pl.BlockSpec((tm, D), lambda i, j: (i, 0)),
            pl.BlockSpec((None, 1, D), lambda i, j: (layer, 0, 0)),
            pl.BlockSpec((None, D, tn), lambda i, j: (layer, 0, j)),
        ],
        out_specs=pl.BlockSpec((tn // dh, tm, dh), lambda i, j: (j, i, 0)),
        out_shape=jax.ShapeDtypeStruct((N // dh, S, dh), BF16),
        scratch_shapes=[pltpu.VMEM((tm, D), BF16)],
        compiler_params=_params(_ARB2),
        name="moba_qkv",
    )(h, ln, w)


def _tile_update(s_ref, v, m_ref, acc_ref, p_ref, r0, *, shift=None, causal=False):
    rows, tk = s_ref.shape[0] - r0, s_ref.shape[1]
    for c0 in range(r0, r0 + rows, ATTN_ROWS):
        c1 = c0 + ATTN_ROWS
        s = s_ref[c0:c1, :]
        if causal:
            qpos = lax.broadcasted_iota(jnp.int32, s.shape, 0) + (c0 - r0)
            kpos = lax.broadcasted_iota(jnp.int32, s.shape, 1)
            s = jnp.where(kpos <= qpos, s, -jnp.inf)
        m_old = m_ref[c0:c1, :]
        row_max = jnp.max(s, axis=-1, keepdims=True)
        if shift is not None:
            row_max = row_max - shift
        m_new = jnp.maximum(m_old, row_max)
        m_ref[c0:c1, :] = m_new
        sub = m_new if shift is None else m_new + shift
        p_ref[c0:c1, :] = jnp.exp2(s - jnp.tile(sub, (1, tk // LANES))).astype(BF16)
        alpha = jnp.exp2(m_old - m_new)
        acc_ref[c0:c1, :] = jnp.tile(alpha, (1, acc_ref.shape[1] // LANES)) * acc_ref[c0:c1, :]
    acc_ref[r0:, :] += _dot(p_ref[r0:, :], v)


def _causal_attention(qi, score_fn, va_ref, s_refs, m_ref, acc_ref, p_ref, o_ref,
                      *, tq, tk, slope2=None):
    per = tq // tk
    assert per % 2 == 0
    dv = o_ref.shape[1]
    m_ref[...] = jnp.full_like(m_ref, M_INIT)
    acc_ref[...] = jnp.zeros_like(acc_ref)
    s_refs[0][...] = score_fn(0, 0)

    def tile(t, rd, wr):
        start = pl.multiple_of(t * tk, tk)
        wr[...] = score_fn(0, start + tk)
        shift = None if slope2 is None else slope2 * (qi * tq - t * tk).astype(F32)
        _tile_update(rd, va_ref[pl.ds(start, tk), :], m_ref, acc_ref, p_ref, 0, shift=shift)

    def pair(j):
        tile(2 * j, s_refs[0], s_refs[1])
        tile(2 * j + 1, s_refs[1], s_refs[0])

    def quad(j, carry):
        pair(2 * j)
        pair(2 * j + 1)
        return carry

    n_pairs = qi * (per // 2)
    lax.fori_loop(0, n_pairs // 2, quad, 0)

    @pl.when(n_pairs % 2 == 1)
    def _():
        pair(n_pairs - 1)

    for b in range(per):
        r0 = b * tk
        start = pl.multiple_of(qi * tq + r0, tk)
        rd, wr = s_refs[b % 2], s_refs[1 - b % 2]
        if b + 1 < per:
            wr[r0 + tk:, :] = score_fn(r0 + tk, start + tk)
        shift = None if slope2 is None else slope2 * float(-r0)
        _tile_update(rd, va_ref[pl.ds(start, tk), :], m_ref, acc_ref, p_ref, r0,
                     shift=shift, causal=True)

    o_ref[...] = (acc_ref[:, :dv] / acc_ref[:, dv:]).astype(BF16)


def _attn_scratch(S, tq, tk, dv):
    return [
        pltpu.VMEM((S, dv + LANES), BF16),
        pltpu.VMEM((tq, tk), F32),
        pltpu.VMEM((tq, tk), F32),
        pltpu.VMEM((tq, LANES), F32),
        pltpu.VMEM((tq, dv + LANES), F32),
        pltpu.VMEM((tq, tk), BF16),
    ]


def _fill_values(va_ref, v_ref):
    dv = v_ref.shape[1]
    va_ref[:, :dv] = v_ref[...]
    va_ref[:, dv:] = jnp.ones((va_ref.shape[0], va_ref.shape[1] - dv), BF16)


def _split3(x):
    hi = x.astype(BF16).astype(F32)
    mid = (x - hi).astype(BF16).astype(F32)
    lo = ((x - hi) - mid).astype(BF16).astype(F32)
    return hi, mid, lo


def _moba_kernel(slopes_ref, q_ref, k_ref, v_ref, o_ref,
                 kaug_ref, kmean_ref, qaug_ref, va_ref, s0_ref, s1_ref, m_ref, acc_ref, p_ref,
                 *, nb, bs, topk, tq, tk):
    h = pl.program_id(0)
    qi = pl.program_id(1)
    slope2 = slopes_ref[h] * LOG2E
    dh = q_ref.shape[1]
    nbp = -(-nb // 8) * 8
    assert nbp + 8 <= LANES

    @pl.when(qi == 0)
    def _():
        _fill_values(va_ref, v_ref)
        kaug_ref[:, :dh] = k_ref[...]
        lane_b = lax.broadcasted_iota(jnp.int32, (bs, LANES), 1)
        row_b = lax.broadcasted_iota(jnp.int32, (bs, LANES), 0)
        kmean_ref[...] = jnp.zeros_like(kmean_ref)
        for n in range(nb):
            hi, mid, lo = _split3(slope2 * (row_b + (n * bs) % tk).astype(F32))
            extra = jnp.where(lane_b == n, 1.0, 0.0)
            extra = jnp.where(lane_b == nbp, hi, extra)
            extra = jnp.where(lane_b == nbp + 1, mid, extra)
            extra = jnp.where(lane_b == nbp + 2, lo, extra)
            kaug_ref[n * bs:(n + 1) * bs, dh:] = extra.astype(BF16)
            kmean_ref[n:n + 1, :] = jnp.mean(
                k_ref[n * bs:(n + 1) * bs, :].astype(F32), axis=0, keepdims=True)

    q = q_ref[...]

    gate = _dot_nt(kmean_ref[...].astype(BF16), q)[:nbp, :]
    blk = lax.broadcasted_iota(jnp.int32, (nbp, tq), 0)
    cur = qi * (tq // bs) + lax.broadcasted_iota(jnp.int32, (nbp, tq), 1) // bs
    valid = blk < cur
    blk_f = blk.astype(F32)
    g = jnp.where(valid, gate, -jnp.inf)
    picked = jnp.zeros((nbp, tq), jnp.bool_)
    for _ in range(topk):
        mx = jnp.max(g, axis=0, keepdims=True)
        first = jnp.min(jnp.where(g == mx, blk_f, float(nbp)), axis=0, keepdims=True)
        hit = blk_f == first
        picked = jnp.logical_or(picked, hit)
        g = jnp.where(hit, -jnp.inf, g)
    attend = jnp.logical_or(jnp.logical_and(picked, valid), blk == cur)
    extra_t = jnp.concatenate(
        [jnp.where(attend, 0.0, MASK_BIAS),
         jnp.ones((8, tq), F32),
         jnp.zeros((LANES - nbp - 8, tq), F32)], axis=0)
    qaug_ref[...] = jnp.concatenate([q, extra_t.T.astype(BF16)], axis=1)

    def score_fn(r0, start):
        return _dot_nt(qaug_ref[r0:, :], kaug_ref[pl.ds(start, tk), :])

    _causal_attention(qi, score_fn, va_ref, (s0_ref, s1_ref), m_ref, acc_ref, p_ref, o_ref,
                      tq=tq, tk=tk, slope2=slope2)


def _moba_attention(qkv, slopes, *, tq=ATTN_TQ, tk=ATTN_TK):
    H = N_HEADS
    _, S, dh = qkv.shape
    bs = MOBA_BLOCK
    nb = S // bs
    assert S % tq == 0 and tq % tk == 0 and tk % bs == 0 and nb <= LANES and dh == LANES
    return pl.pallas_call(
        functools.partial(_moba_kernel, nb=nb, bs=bs, topk=min(MOBA_TOPK, nb), tq=tq, tk=tk),
        grid=(H, S // tq),
        in_specs=[
            pl.BlockSpec(memory_space=pltpu.SMEM),
            pl.BlockSpec((None, tq, dh), lambda h, i: (h, i, 0)),
            pl.BlockSpec((None, S, dh), lambda h, i: (H + h, 0, 0)),
            pl.BlockSpec((None, S, dh), lambda h, i: (2 * H + h, 0, 0)),
        ],
        out_specs=pl.BlockSpec((tq, dh), lambda h, i: (i, h)),
        out_shape=jax.ShapeDtypeStruct((S, H * dh), BF16),
        scratch_shapes=[
            pltpu.VMEM((S, dh + LANES), BF16),
            pltpu.VMEM((LANES, dh), F32),
            pltpu.VMEM((tq, dh + LANES), BF16),
        ] + _attn_scratch(S, tq, tk, dh),
        compiler_params=_params(_ARB2),
        name="moba_attn",
    )(slopes, qkv, qkv, qkv)


def _oproj_kernel(h_ref, o_ref, w_ref, out_ref):
    out_ref[...] = h_ref[...] + _dot(o_ref[...], w_ref[...])


def _oproj(h, o, w, layer, *, tm=512):
    S, D = h.shape
    K = o.shape[1]
    return pl.pallas_call(
        _oproj_kernel,
        grid=(S // tm,),
        in_specs=[
            pl.BlockSpec((tm, D), lambda i: (i, 0)),
            pl.BlockSpec((tm, K), lambda i: (i, 0)),
            pl.BlockSpec((None, K, D), lambda i: (layer, 0, 0)),
        ],
        out_specs=pl.BlockSpec((tm, D), lambda i: (i, 0)),
        out_shape=jax.ShapeDtypeStruct((S, D), F32),
        compiler_params=_params(_ARB1),
        name="oproj",
    )(h, o, w)


def _rope_pair(pair, cs):
    y = pair * cs
    z = y + pltpu.roll(y, LANES // 2, 1)
    lane = lax.broadcasted_iota(jnp.int32, y.shape, 1)
    return jnp.where(lane < LANES // 2, z, 0.0)


def _mla_kv_kernel(h_ref, g_ref, wd_ref, cg_ref, wu_ref, cs_ref, k_ref, v_ref, *, nh):
    hn = _rms(h_ref[...], g_ref[...]).astype(BF16)
    t = _dot(hn, wd_ref[...])
    c = _rms(t[:, :KV_LORA], cg_ref[...]).astype(BF16)
    k_rope = _rope_pair(t[:, KV_LORA:], cs_ref[...])
    for hd in range(nh):
        a = _dot(c, wu_ref[:, hd * (QK_NOPE + V_DIM):(hd + 1) * (QK_NOPE + V_DIM)])
        k_ref[hd] = jnp.concatenate([a[:, :QK_NOPE], k_rope], axis=1).astype(BF16)
        v_ref[hd] = a[:, QK_NOPE:].astype(BF16)


def _mla_kv(h, g, wd, cg, wu, cs, *, tm=512):
    S, D = h.shape
    H = N_HEADS
    return pl.pallas_call(
        functools.partial(_mla_kv_kernel, nh=H),
        grid=(S // tm,),
        in_specs=[
            pl.BlockSpec((tm, D), lambda i: (i, 0)),
            pl.BlockSpec((1, D), lambda i: (0, 0)),
            pl.BlockSpec(wd.shape, lambda i: (0, 0)),
            pl.BlockSpec((1, KV_LORA), lambda i: (0, 0)),
            pl.BlockSpec(wu.shape, lambda i: (0, 0)),
            pl.BlockSpec((tm, LANES), lambda i: (i, 0)),
        ],
        out_specs=[
            pl.BlockSpec((H, tm, HEAD_PAD), lambda i: (0, i, 0)),
            pl.BlockSpec((H, tm, V_DIM), lambda i: (0, i, 0)),
        ],
        out_shape=[
            jax.ShapeDtypeStruct((H, S, HEAD_PAD), BF16),
            jax.ShapeDtypeStruct((H, S, V_DIM), BF16),
        ],
        compiler_params=_params(_ARB1),
        name="mla_kv",
    )(h, g, wd, cg, wu, cs)


def _mla_q_kernel(h_ref, g_ref, wd_ref, cg_ref, wu_ref, cs_ref, q_ref, *, nh, scale):
    xn = _rms(h_ref[...], g_ref[...]).astype(BF16)
    cq = _rms(_dot(xn, wd_ref[...]), cg_ref[...]).astype(BF16)
    cs = cs_ref[...]
    for hd in range(nh):
        a = _dot(cq, wu_ref[:, hd * HEAD_PAD:(hd + 1) * HEAD_PAD])
        q_rope = _rope_pair(a[:, QK_NOPE:], cs)
        q_ref[hd] = (jnp.concatenate([a[:, :QK_NOPE], q_rope], axis=1) * scale).astype(BF16)


def _mla_q(h, ln, wd, cg, wu, cs, layer, j, *, tm=512):
    S, D = h.shape
    H = N_HEADS
    R = wd.shape[2]
    return pl.pallas_call(
        functools.partial(_mla_q_kernel, nh=H, scale=LOG2E * float(QK_NOPE + QK_ROPE) ** -0.5),
        grid=(S // tm,),
        in_specs=[
            pl.BlockSpec((tm, D), lambda i: (i, 0)),
            pl.BlockSpec((None, 1, D), lambda i: (layer, 0, 0)),
            pl.BlockSpec((None, D, R), lambda i: (j, 0, 0)),
            pl.BlockSpec((None, 1, R), lambda i: (j, 0, 0)),
            pl.BlockSpec((None, R, H * HEAD_PAD), lambda i: (j, 0, 0)),
            pl.BlockSpec((tm, LANES), lambda i: (i, 0)),
        ],
        out_specs=pl.BlockSpec((H, tm, HEAD_PAD), lambda i: (0, i, 0)),
        out_shape=jax.ShapeDtypeStruct((H, S, HEAD_PAD), BF16),
        compiler_params=_params(_ARB1),
        name="mla_q",
    )(h, ln, wd, cg, wu, cs)


def _mla_attn_kernel(q_ref, k_ref, v_ref, o_ref, va_ref, s0_ref, s1_ref, m_ref, acc_ref, p_ref,
                     *, tq, tk):
    qi = pl.program_id(1)

    @pl.when(qi == 0)
    def _():
        _fill_values(va_ref, v_ref)

    def score_fn(r0, start):
        return _dot_nt(q_ref[r0:, :], k_ref[pl.ds(start, tk), :])

    _causal_attention(qi, score_fn, va_ref, (s0_ref, s1_ref), m_ref, acc_ref, p_ref, o_ref,
                      tq=tq, tk=tk)


def _mla_attention(q, k, v, *, tq=ATTN_TQ, tk=ATTN_TK):
    H, S, _ = q.shape
    assert S % tq == 0 and tq % tk == 0
    return pl.pallas_call(
        functools.partial(_mla_attn_kernel, tq=tq, tk=tk),
        grid=(H, S // tq),
        in_specs=[
            pl.BlockSpec((None, tq, HEAD_PAD), lambda h, i: (h, i, 0)),
            pl.BlockSpec((None, S, HEAD_PAD), lambda h, i: (h, 0, 0)),
            pl.BlockSpec((None, S, V_DIM), lambda h, i: (h, 0, 0)),
        ],
        out_specs=pl.BlockSpec((tq, V_DIM), lambda h, i: (i, h)),
        out_shape=jax.ShapeDtypeStruct((S, H * V_DIM), BF16),
        scratch_shapes=_attn_scratch(S, tq, tk, V_DIM),
        compiler_params=_params(_ARB2),
        name="mla_attn",
    )(q, k, v)


def _rotate_half_cols(w):
    w1, w2 = jnp.split(w, 2, axis=-1)
    return jnp.concatenate([-w2, w1], axis=-1)


def _alibi_slopes(n):
    return jnp.asarray(np.array([2.0 ** (-8.0 * (i + 1) / n) for i in range(n)], dtype=np.float32))


def _rope_table(s):
    inv = 1.0 / (ROPE_THETA ** (jnp.arange(0, QK_ROPE, 2, dtype=F32) / QK_ROPE))
    ang = jnp.arange(s, dtype=F32)[:, None] * inv[None, :]
    cos, sin = jnp.cos(ang), jnp.sin(ang)
    return jnp.concatenate([cos, cos, sin, sin], axis=-1)


def kernel(x, ln_ffn1, ffn1_wgu, ffn1_wd, ln_mix, ln_ffn2, ffn2_wgu, ffn2_wd, moba_wqkv, moba_wo, kv_norm, mla_wdkv, ckv_norm, mla_wukv, mla_wdq, cq_norm, mla_wuq, mla_wo, final_norm):
    B, S, D = x.shape
    assert B == 1
    depth = ln_ffn1.shape[0]
    n_a = moba_wqkv.shape[0]
    n_b = mla_wdq.shape[0]
    H = N_HEADS

    ln1 = ln_ffn1.reshape(depth, 1, D)
    lnm = ln_mix.reshape(depth, 1, D)
    ln2 = ln_ffn2.reshape(depth, 1, D)
    wqkv, wo_a = moba_wqkv, moba_wo.astype(BF16)
    wo_b = mla_wo.astype(BF16)
    slopes = _alibi_slopes(H)
    cs = _rope_table(S)

    kr = mla_wdkv[:, KV_LORA:]
    wdkv = jnp.concatenate([mla_wdkv, _rotate_half_cols(kr)], axis=-1).astype(BF16)
    wukv = mla_wukv.astype(BF16)
    wdq = mla_wdq.astype(BF16)
    R = mla_wuq.shape[1]
    wuq4 = mla_wuq.reshape(n_b, R, H, QK_NOPE + QK_ROPE)
    rope_cols = wuq4[..., QK_NOPE:]
    wuq = jnp.concatenate([wuq4, _rotate_half_cols(rope_cols)], axis=-1)
    wuq = wuq.reshape(n_b, R, H * HEAD_PAD).astype(BF16)
    cqn = cq_norm.reshape(n_b, 1, R)

    h = x.reshape(S, D)

    fg = final_norm.reshape(1, D)

    def macaron(h, l, mixer):
        h = _ffn(h, ln1, ffn1_wgu, ffn1_wd, l)
        h = mixer(h)
        return _ffn(h, ln2, ffn2_wgu, ffn2_wd, l, final_gain=fg if l == depth - 1 else None)

    for l in range(n_a):
        def moba(h, l=l):
            qkv = _moba_qkv(h, lnm, wqkv, l)
            o = _moba_attention(qkv, slopes)
            return _oproj(h, o, wo_a, l)
        h = macaron(h, l, moba)

    k_sh, v_sh = _mla_kv(h, kv_norm.reshape(1, D), wdkv, ckv_norm.reshape(1, KV_LORA), wukv, cs)

    for j in range(n_b):
        l = n_a + j

        def mla(h, l=l, j=j):
            q = _mla_q(h, lnm, wdq, cqn, wuq, cs, l, j)
            o = _mla_attention(q, k_sh, v_sh)
            return _oproj(h, o, wo_b, j)
        h = macaron(h, l, mla)

    return h.reshape(B, S, D)
```

```python
import functools

import numpy as np
import jax
import jax.numpy as jnp
from jax import lax
from jax.experimental import pallas as pl
from jax.experimental.pallas import tpu as pltpu

F32 = jnp.float32
BF16 = jnp.bfloat16

EPS = 1e-6
ROPE_THETA = 10000.0

N_HEADS = 16
MOBA_BLOCK = 256
MOBA_TOPK = 3
KV_LORA = 512
QK_NOPE = 128
QK_ROPE = 64
V_DIM = 128

LANES = 128
HEAD_PAD = 2 * LANES
MASK_BIAS = -1e9
M_INIT = -1e30
LOG2E = 1.4426950408889634
VMEM_LIMIT = 56 * 1024 * 1024

ATTN_TQ = 1024
ATTN_TK = 512
ATTN_ROWS = 256

_ARB1 = ("arbitrary",)
_ARB2 = ("arbitrary", "arbitrary")


def _params(sem):
    return pltpu.CompilerParams(dimension_semantics=sem, vmem_limit_bytes=VMEM_LIMIT)


def _rms(x, g):
    ms = jnp.mean(x * x, axis=-1, keepdims=True)
    return x * lax.rsqrt(ms + EPS) * g


def _dot(a, b):
    return jnp.dot(a, b, preferred_element_type=F32)


def _dot_nt(a, b):
    return lax.dot_general(a, b, (((1,), (1,)), ((), ())), preferred_element_type=F32)


def _ffn_kernel(h_ref, g_ref, wgu_hbm, wd_hbm, *rest, layer, nf, tf, final):
    fg_ref = rest[0] if final else None
    o_ref, xn_ref, wg_buf, wu_buf, wd_buf, sem = rest[-6:]
    i = pl.program_id(0)
    F = nf * tf
    assert nf % 2 == 0

    def tile_copies(f, slot):
        col = f * tf
        return (
            pltpu.make_async_copy(wgu_hbm.at[layer, :, pl.ds(col, tf)], wg_buf.at[slot], sem.at[0, slot]),
            pltpu.make_async_copy(wgu_hbm.at[layer, :, pl.ds(F + col, tf)], wu_buf.at[slot], sem.at[1, slot]),
            pltpu.make_async_copy(wd_hbm.at[layer, pl.ds(col, tf), :], wd_buf.at[slot], sem.at[2, slot]),
        )

    def request(f, slot):
        for c in tile_copies(f, slot):
            c.start()

    def arrive(f, slot):
        for c in tile_copies(f, slot):
            c.wait()

    def half_step(xn, slot):
        gate = _dot(xn, wg_buf[slot].astype(BF16))
        up = _dot(xn, wu_buf[slot].astype(BF16))
        act = (gate * jax.nn.sigmoid(gate) * (0.5 * up)).astype(BF16)
        return _dot(act, wd_buf[slot].astype(BF16))

    @pl.when(i == 0)
    def _():
        request(0, 0)

    arrive(0, 0)
    request(1, 1)
    h = h_ref[...]
    xn = _rms(h, g_ref[...]).astype(BF16)
    xn_ref[...] = xn
    o_ref[...] = h + half_step(xn, 0)

    def two_tiles(j, carry):
        f = 1 + 2 * j
        arrive(f, 1)
        request(f + 1, 0)
        o_ref[...] += half_step(xn_ref[...], 1)
        arrive(f + 1, 0)
        request(f + 2, 1)
        o_ref[...] += half_step(xn_ref[...], 0)
        return carry

    lax.fori_loop(0, (nf - 2) // 2, two_tiles, 0)

    arrive(nf - 1, 1)

    @pl.when(i + 1 < pl.num_programs(0))
    def _():
        request(0, 0)

    o_ref[...] += half_step(xn_ref[...], 1)
    if final:
        o_ref[...] = _rms(o_ref[...], fg_ref[...])


def _ffn(h, ln, wgu, wd, layer, *, final_gain=None, tm=1024, tf=256):
    S, D = h.shape
    F = wd.shape[1]
    nf = F // tf
    final = final_gain is not None
    in_specs = [
        pl.BlockSpec((tm, D), lambda i: (i, 0)),
        pl.BlockSpec((None, 1, D), lambda i: (layer, 0, 0)),
        pl.BlockSpec(memory_space=pl.ANY),
        pl.BlockSpec(memory_space=pl.ANY),
    ]
    args = [h, ln, wgu, wd]
    if final:
        in_specs.append(pl.BlockSpec((1, D), lambda i: (0, 0)))
        args.append(final_gain)
    return pl.pallas_call(
        functools.partial(_ffn_kernel, layer=layer, nf=nf, tf=tf, final=final),
        grid=(S // tm,),
        in_specs=in_specs,
        out_specs=pl.BlockSpec((tm, D), lambda i: (i, 0)),
        out_shape=jax.ShapeDtypeStruct((S, D), F32),
        scratch_shapes=[
            pltpu.VMEM((tm, D), BF16),
            pltpu.VMEM((2, D, tf), F32),
            pltpu.VMEM((2, D, tf), F32),
            pltpu.VMEM((2, tf, D), F32),
            pltpu.SemaphoreType.DMA((3, 2)),
        ],
        compiler_params=_params(_ARB1),
        name="ffn_final" if final else "ffn",
    )(*args)


def _qkv_kernel(h_ref, g_ref, w_ref, o_ref, xn_ref, *, n_q_blocks, scale):
    j = pl.program_id(1)
    heads, _, dh = o_ref.shape

    def project(xn):
        acc = _dot(xn, w_ref[...].astype(BF16))
        acc = (acc * jnp.where(j < n_q_blocks, scale, 1.0)).astype(BF16)
        for hd in range(heads):
            o_ref[hd] = acc[:, hd * dh:(hd + 1) * dh]

    @pl.when(j == 0)
    def _():
        xn = _rms(h_ref[...], g_ref[...]).astype(BF16)
        xn_ref[...] = xn
        project(xn)

    @pl.when(j > 0)
    def _():
        project(xn_ref[...])


def _moba_qkv(h, ln, w, layer, *, tm=1024, tn=1024):
    S, D = h.shape
    N = w.shape[2]
    dh = D // N_HEADS
    return pl.pallas_call(
        functools.partial(_qkv_kernel, n_q_blocks=D // tn, scale=LOG2E * float(dh) ** -0.5),
        grid=(S // tm, N // tn),
        in_specs=[
            pl.BlockSpec((tm, D), lambda i, j: (i, 0)),
            pl.BlockSpec((None, 1, D), lambda i, j: (layer, 0, 0)),
            pl.BlockSpec((None, D, tn), lambda i, j: (layer, 0, j)),
        ],
        out_specs=pl.BlockSpec((tn // dh, tm, dh), lambda i, j: (j, i, 0)),
        out_shape=jax.ShapeDtypeStruct((N // dh, S, dh), BF16),
        scratch_shapes=[pltpu.VMEM((tm, D), BF16)],
        compiler_params=_params(_ARB2),
        name="moba_qkv",
    )(h, ln, w)


def _tile_update(s_ref, v, m_ref, acc_ref, p_ref, r0, *, shift=None, causal=False):
    rows, tk = s_ref.shape[0] - r0, s_ref.shape[1]
    for c0 in range(r0, r0 + rows, ATTN_ROWS):
        c1 = c0 + ATTN_ROWS
        s = s_ref[c0:c1, :]
        if causal:
            qpos = lax.broadcasted_iota(jnp.int32, s.shape, 0) + (c0 - r0)
            kpos = lax.broadcasted_iota(jnp.int32, s.shape, 1)
            s = jnp.where(kpos <= qpos, s, -jnp.inf)
        m_old = m_ref[c0:c1, :]
        row_max = jnp.max(s, axis=-1, keepdims=True)
        if shift is not None:
            row_max = row_max - shift
        m_new = jnp.maximum(m_old, row_max)
        m_ref[c0:c1, :] = m_new
        sub = m_new if shift is None else m_new + shift
        p_ref[c0:c1, :] = jnp.exp2(s - jnp.tile(sub, (1, tk // LANES))).astype(BF16)
        alpha = jnp.exp2(m_old - m_new)
        acc_ref[c0:c1, :] = jnp.tile(alpha, (1, acc_ref.shape[1] // LANES)) * acc_ref[c0:c1, :]
    acc_ref[r0:, :] += _dot(p_ref[r0:, :], v)


def _causal_attention(qi, score_fn, va_ref, s_refs, m_ref, acc_ref, p_ref, o_ref,
                      *, tq, tk, slope2=None):
    per = tq // tk
    assert per % 2 == 0
    dv = o_ref.shape[1]
    m_ref[...] = jnp.full_like(m_ref, M_INIT)
    acc_ref[...] = jnp.zeros_like(acc_ref)
    s_refs[0][...] = score_fn(0, 0)

    def tile(t, rd, wr):
        start = pl.multiple_of(t * tk, tk)
        wr[...] = score_fn(0, start + tk)
        shift = None if slope2 is None else slope2 * (qi * tq - t * tk).astype(F32)
        _tile_update(rd, va_ref[pl.ds(start, tk), :], m_ref, acc_ref, p_ref, 0, shift=shift)

    def pair(j):
        tile(2 * j, s_refs[0], s_refs[1])
        tile(2 * j + 1, s_refs[1], s_refs[0])

    def quad(j, carry):
        pair(2 * j)
        pair(2 * j + 1)
        return carry

    n_pairs = qi * (per // 2)
    lax.fori_loop(0, n_pairs // 2, quad, 0)

    @pl.when(n_pairs % 2 == 1)
    def _():
        pair(n_pairs - 1)

    for b in range(per):
        r0 = b * tk
        start = pl.multiple_of(qi * tq + r0, tk)
        rd, wr = s_refs[b % 2], s_refs[1 - b % 2]
        if b + 1 < per:
            wr[r0 + tk:, :] = score_fn(r0 + tk, start + tk)
        shift = None if slope2 is None else slope2 * float(-r0)
        _tile_update(rd, va_ref[pl.ds(start, tk), :], m_ref, acc_ref, p_ref, r0,
                     shift=shift, causal=True)

    o_ref[...] = (acc_ref[:, :dv] / acc_ref[:, dv:]).astype(BF16)


def _attn_scratch(S, tq, tk, dv):
    return [
        pltpu.VMEM((S, dv + LANES), BF16),
        pltpu.VMEM((tq, tk), F32),
        pltpu.VMEM((tq, tk), F32),
        pltpu.VMEM((tq, LANES), F32),
        pltpu.VMEM((tq, dv + LANES), F32),
        pltpu.VMEM((tq, tk), BF16),
    ]


def _fill_values(va_ref, v_ref):
    dv = v_ref.shape[1]
    va_ref[:, :dv] = v_ref[...]
    va_ref[:, dv:] = jnp.ones((va_ref.shape[0], va_ref.shape[1] - dv), BF16)


def _split3(x):
    hi = x.astype(BF16).astype(F32)
    mid = (x - hi).astype(BF16).astype(F32)
    lo = ((x - hi) - mid).astype(BF16).astype(F32)
    return hi, mid, lo


def _moba_kernel(slopes_ref, q_ref, k_ref, v_ref, o_ref,
                 kaug_ref, kmean_ref, qaug_ref, va_ref, s0_ref, s1_ref, m_ref, acc_ref, p_ref,
                 *, nb, bs, topk, tq, tk):
    h = pl.program_id(0)
    qi = pl.program_id(1)
    slope2 = slopes_ref[h] * LOG2E
    dh = q_ref.shape[1]
    nbp = -(-nb // 8) * 8
    assert nbp + 8 <= LANES

    @pl.when(qi == 0)
    def _():
        _fill_values(va_ref, v_ref)
        kaug_ref[:, :dh] = k_ref[...]
        lane_b = lax.broadcasted_iota(jnp.int32, (bs, LANES), 1)
        row_b = lax.broadcasted_iota(jnp.int32, (bs, LANES), 0)
        kmean_ref[...] = jnp.zeros_like(kmean_ref)
        for n in range(nb):
            hi, mid, lo = _split3(slope2 * (row_b + (n * bs) % tk).astype(F32))
            extra = jnp.where(lane_b == n, 1.0, 0.0)
            extra = jnp.where(lane_b == nbp, hi, extra)
            extra = jnp.where(lane_b == nbp + 1, mid, extra)
            extra = jnp.where(lane_b == nbp + 2, lo, extra)
            kaug_ref[n * bs:(n + 1) * bs, dh:] = extra.astype(BF16)
            kmean_ref[n:n + 1, :] = jnp.mean(
                k_ref[n * bs:(n + 1) * bs, :].astype(F32), axis=0, keepdims=True)

    q = q_ref[...]

    gate = _dot_nt(kmean_ref[...].astype(BF16), q)[:nbp, :]
    blk = lax.broadcasted_iota(jnp.int32, (nbp, tq), 0)
    cur = qi * (tq // bs) + lax.broadcasted_iota(jnp.int32, (nbp, tq), 1) // bs
    valid = blk < cur
    blk_f = blk.astype(F32)
    g = jnp.where(valid, gate, -jnp.inf)
    picked = jnp.zeros((nbp, tq), jnp.bool_)
    for _ in range(topk):
        mx = jnp.max(g, axis=0, keepdims=True)
        first = jnp.min(jnp.where(g == mx, blk_f, float(nbp)), axis=0, keepdims=True)
        hit = blk_f == first
        picked = jnp.logical_or(picked, hit)
        g = jnp.where(hit, -jnp.inf, g)
    attend = jnp.logical_or(jnp.logical_and(picked, valid), blk == cur)
    extra_t = jnp.concatenate(
        [jnp.where(attend, 0.0, MASK_BIAS),
         jnp.ones((8, tq), F32),
         jnp.zeros((LANES - nbp - 8, tq), F32)], axis=0)
    qaug_ref[...] = jnp.concatenate([q, extra_t.T.astype(BF16)], axis=1)

    def score_fn(r0, start):
        return _dot_nt(qaug_ref[r0:, :], kaug_ref[pl.ds(start, tk), :])

    _causal_attention(qi, score_fn, va_ref, (s0_ref, s1_ref), m_ref, acc_ref, p_ref, o_ref,
                      tq=tq, tk=tk, slope2=slope2)


def _moba_attention(qkv, slopes, *, tq=ATTN_TQ, tk=ATTN_TK):
    H = N_HEADS
    _, S, dh = qkv.shape
    bs = MOBA_BLOCK
    nb = S // bs
    assert S % tq == 0 and tq % tk == 0 and tk % bs == 0 and nb <= LANES and dh == LANES
    return pl.pallas_call(
        functools.partial(_moba_kernel, nb=nb, bs=bs, topk=min(MOBA_TOPK, nb), tq=tq, tk=tk),
        grid=(H, S // tq),
        in_specs=[
            pl.BlockSpec(memory_space=pltpu.SMEM),
            pl.BlockSpec((None, tq, dh), lambda h, i: (h, i, 0)),
            pl.BlockSpec((None, S, dh), lambda h, i: (H + h, 0, 0)),
            pl.BlockSpec((None, S, dh), lambda h, i: (2 * H + h, 0, 0)),
        ],
        out_specs=pl.BlockSpec((tq, dh), lambda h, i: (i, h)),
        out_shape=jax.ShapeDtypeStruct((S, H * dh), BF16),
        scratch_shapes=[
            pltpu.VMEM((S, dh + LANES), BF16),
            pltpu.VMEM((LANES, dh), F32),
            pltpu.VMEM((tq, dh + LANES), BF16),
        ] + _attn_scratch(S, tq, tk, dh),
        compiler_params=_params(_ARB2),
        name="moba_attn",
    )(slopes, qkv, qkv, qkv)


def _oproj_kernel(h_ref, o_ref, w_ref, out_ref):
    out_ref[...] = h_ref[...] + _dot(o_ref[...], w_ref[...])


def _oproj(h, o, w, layer, *, tm=512):
    S, D = h.shape
    K = o.shape[1]
    return pl.pallas_call(
        _oproj_kernel,
        grid=(S // tm,),
        in_specs=[
            pl.BlockSpec((tm, D), lambda i: (i, 0)),
            pl.BlockSpec((tm, K), lambda i: (i, 0)),
            pl.BlockSpec((None, K, D), lambda i: (layer, 0, 0)),
        ],
        out_specs=pl.BlockSpec((tm, D), lambda i: (i, 0)),
        out_shape=jax.ShapeDtypeStruct((S, D), F32),
        compiler_params=_params(_ARB1),
        name="oproj",
    )(h, o, w)


def _rope_pair(pair, cs):
    y = pair * cs
    z = y + pltpu.roll(y, LANES // 2, 1)
    lane = lax.broadcasted_iota(jnp.int32, y.shape, 1)
    return jnp.where(lane < LANES // 2, z, 0.0)


def _mla_kv_kernel(h_ref, g_ref, wd_ref, cg_ref, wu_ref, cs_ref, k_ref, v_ref, *, nh):
    hn = _rms(h_ref[...], g_ref[...]).astype(BF16)
    t = _dot(hn, wd_ref[...])
    c = _rms(t[:, :KV_LORA], cg_ref[...]).astype(BF16)
    k_rope = _rope_pair(t[:, KV_LORA:], cs_ref[...])
    for hd in range(nh):
        a = _dot(c, wu_ref[:, hd * (QK_NOPE + V_DIM):(hd + 1) * (QK_NOPE + V_DIM)])
        k_ref[hd] = jnp.concatenate([a[:, :QK_NOPE], k_rope], axis=1).astype(BF16)
        v_ref[hd] = a[:, QK_NOPE:].astype(BF16)


def _mla_kv(h, g, wd, cg, wu, cs, *, tm=512):
    S, D = h.shape
    H = N_HEADS
    return pl.pallas_call(
        functools.partial(_mla_kv_kernel, nh=H),
        grid=(S // tm,),
        in_specs=[
            pl.BlockSpec((tm, D), lambda i: (i, 0)),
            pl.BlockSpec((1, D), lambda i: (0, 0)),
            pl.BlockSpec(wd.shape, lambda i: (0, 0)),
            pl.BlockSpec((1, KV_LORA), lambda i: (0, 0)),
            pl.BlockSpec(wu.shape, lambda i: (0, 0)),
            pl.BlockSpec((tm, LANES), lambda i: (i, 0)),
        ],
        out_specs=[
            pl.BlockSpec((H, tm, HEAD_PAD), lambda i: (0, i, 0)),
            pl.BlockSpec((H, tm, V_DIM), lambda i: (0, i, 0)),
        ],
        out_shape=[
            jax.ShapeDtypeStruct((H, S, HEAD_PAD), BF16),
            jax.ShapeDtypeStruct((H, S, V_DIM), BF16),
        ],
        compiler_params=_params(_ARB1),
        name="mla_kv",
    )(h, g, wd, cg, wu, cs)


def _mla_q_kernel(h_ref, g_ref, wd_ref, cg_ref, wu_ref, cs_ref, q_ref, *, nh, scale):
    xn = _rms(h_ref[...], g_ref[...]).astype(BF16)
    cq = _rms(_dot(xn, wd_ref[...]), cg_ref[...]).astype(BF16)
    cs = cs_ref[...]
    for hd in range(nh):
        a = _dot(cq, wu_ref[:, hd * HEAD_PAD:(hd + 1) * HEAD_PAD])
        q_rope = _rope_pair(a[:, QK_NOPE:], cs)
        q_ref[hd] = (jnp.concatenate([a[:, :QK_NOPE], q_rope], axis=1) * scale).astype(BF16)


def _mla_q(h, ln, wd, cg, wu, cs, layer, j, *, tm=512):
    S, D = h.shape
    H = N_HEADS
    R = wd.shape[2]
    return pl.pallas_call(
        functools.partial(_mla_q_kernel, nh=H, scale=LOG2E * float(QK_NOPE + QK_ROPE) ** -0.5),
        grid=(S // tm,),
        in_specs=[
            pl.BlockSpec((tm, D), lambda i: (i, 0)),
            pl.BlockSpec((None, 1, D), lambda i: (layer, 0, 0)),
            pl.BlockSpec((None, D, R), lambda i: (j, 0, 0)),
            pl.BlockSpec((None, 1, R), lambda i: (j, 0, 0)),
            pl.BlockSpec((None, R, H * HEAD_PAD), lambda i: (j, 0, 0)),
            pl.BlockSpec((tm, LANES), lambda i: (i, 0)),
        ],
        out_specs=pl.BlockSpec((H, tm, HEAD_PAD), lambda i: (0, i, 0)),
        out_shape=jax.ShapeDtypeStruct((H, S, HEAD_PAD), BF16),
        compiler_params=_params(_ARB1),
        name="mla_q",
    )(h, ln, wd, cg, wu, cs)


def _mla_attn_kernel(q_ref, k_ref, v_ref, o_ref, va_ref, s0_ref, s1_ref, m_ref, acc_ref, p_ref,
                     *, tq, tk):
    qi = pl.program_id(1)

    @pl.when(qi == 0)
    def _():
        _fill_values(va_ref, v_ref)

    def score_fn(r0, start):
        return _dot_nt(q_ref[r0:, :], k_ref[pl.ds(start, tk), :])

    _causal_attention(qi, score_fn, va_ref, (s0_ref, s1_ref), m_ref, acc_ref, p_ref, o_ref,
                      tq=tq, tk=tk)


def _mla_attention(q, k, v, *, tq=ATTN_TQ, tk=ATTN_TK):
    H, S, _ = q.shape
    assert S % tq == 0 and tq % tk == 0
    return pl.pallas_call(
        functools.partial(_mla_attn_kernel, tq=tq, tk=tk),
        grid=(H, S // tq),
        in_specs=[
            pl.BlockSpec((None, tq, HEAD_PAD), lambda h, i: (h, i, 0)),
            pl.BlockSpec((None, S, HEAD_PAD), lambda h, i: (h, 0, 0)),
            pl.BlockSpec((None, S, V_DIM), lambda h, i: (h, 0, 0)),
        ],
        out_specs=pl.BlockSpec((tq, V_DIM), lambda h, i: (i, h)),
        out_shape=jax.ShapeDtypeStruct((S, H * V_DIM), BF16),
        scratch_shapes=_attn_scratch(S, tq, tk, V_DIM),
        compiler_params=_params(_ARB2),
        name="mla_attn",
    )(q, k, v)


def _rotate_half_cols(w):
    w1, w2 = jnp.split(w, 2, axis=-1)
    return jnp.concatenate([-w2, w1], axis=-1)


def _alibi_slopes(n):
    return jnp.asarray(np.array([2.0 ** (-8.0 * (i + 1) / n) for i in range(n)], dtype=np.float32))


def _rope_table(s):
    inv = 1.0 / (ROPE_THETA ** (jnp.arange(0, QK_ROPE, 2, dtype=F32) / QK_ROPE))
    ang = jnp.arange(s, dtype=F32)[:, None] * inv[None, :]
    cos, sin = jnp.cos(ang), jnp.sin(ang)
    return jnp.concatenate([cos, cos, sin, sin], axis=-1)


def kernel(x, ln_ffn1, ffn1_wgu, ffn1_wd, ln_mix, ln_ffn2, ffn2_wgu, ffn2_wd, moba_wqkv, moba_wo, kv_norm, mla_wdkv, ckv_norm, mla_wukv, mla_wdq, cq_norm, mla_wuq, mla_wo, final_norm):
    B, S, D = x.shape
    assert B == 1
    depth = ln_ffn1.shape[0]
    n_a = moba_wqkv.shape[0]
    n_b = mla_wdq.shape[0]
    H = N_HEADS

    ln1 = ln_ffn1.reshape(depth, 1, D)
    lnm = ln_mix.reshape(depth, 1, D)
    ln2 = ln_ffn2.reshape(depth, 1, D)
    wqkv, wo_a = moba_wqkv, moba_wo.astype(BF16)
    wo_b = mla_wo.astype(BF16)
    slopes = _alibi_slopes(H)
    cs = _rope_table(S)

    kr = mla_wdkv[:, KV_LORA:]
    wdkv = jnp.concatenate([mla_wdkv, _rotate_half_cols(kr)], axis=-1).astype(BF16)
    wukv = mla_wukv.astype(BF16)
    wdq = mla_wdq.astype(BF16)
    R = mla_wuq.shape[1]
    wuq4 = mla_wuq.reshape(n_b, R, H, QK_NOPE + QK_ROPE)
    rope_cols = wuq4[..., QK_NOPE:]
    wuq = jnp.concatenate([wuq4, _rotate_half_cols(rope_cols)], axis=-1)
    wuq = wuq.reshape(n_b, R, H * HEAD_PAD).astype(BF16)
    cqn = cq_norm.reshape(n_b, 1, R)

    h = x.reshape(S, D)

    fg = final_norm.reshape(1, D)

    def macaron(h, l, mixer):
        h = _ffn(h, ln1, ffn1_wgu, ffn1_wd, l)
        h = mixer(h)
        return _ffn(h, ln2, ffn2_wgu, ffn2_wd, l, final_gain=fg if l == depth - 1 else None)

    for l in range(n_a):
        def moba(h, l=l):
            qkv = _moba_qkv(h, lnm, wqkv, l)
            o = _moba_attention(qkv, slopes)
            return _oproj(h, o, wo_a, l)
        h = macaron(h, l, moba)

    k_sh, v_sh = _mla_kv(h, kv_norm.reshape(1, D), wdkv, ckv_norm.reshape(1, KV_LORA), wukv, cs)

    for j in range(n_b):
        l = n_a + j

        def mla(h, l=l, j=j):
            q = _mla_q(h, lnm, wdq, cqn, wuq, cs, l, j)
            o = _mla_attention(q, k_sh, v_sh)
            return _oproj(h, o, wo_b, j)
        h = macaron(h, l, mla)

    return h.reshape(B, S, D)
```

```python
import functools

import numpy as np
import jax
import jax.numpy as jnp
from jax import lax
from jax.experimental import pallas as pl
from jax.experimental.pallas import tpu as pltpu

F32 = jnp.float32
BF16 = jnp.bfloat16

EPS = 1e-6
ROPE_THETA = 10000.0

N_HEADS = 16
MOBA_BLOCK = 256
MOBA_TOPK = 3
KV_LORA = 512
QK_NOPE = 128
QK_ROPE = 64
V_DIM = 128

LANES = 128
HEAD_PAD = 2 * LANES
MASK_BIAS = -1e9
M_INIT = -1e30
LOG2E = 1.4426950408889634
VMEM_LIMIT = 56 * 1024 * 1024

ATTN_TQ = 1024
ATTN_TK = 512
ATTN_ROWS = 256

_ARB1 = ("arbitrary",)
_ARB2 = ("arbitrary", "arbitrary")


def _params(sem):
    return pltpu.CompilerParams(dimension_semantics=sem, vmem_limit_bytes=VMEM_LIMIT)


def _rms(x, g):
    ms = jnp.mean(x * x, axis=-1, keepdims=True)
    return x * lax.rsqrt(ms + EPS) * g


def _dot(a, b):
    return jnp.dot(a, b, preferred_element_type=F32)


def _dot_nt(a, b):
    return lax.dot_general(a, b, (((1,), (1,)), ((), ())), preferred_element_type=F32)


def _ffn_kernel(h_ref, g_ref, wgu_hbm, wd_hbm, *rest, layer, nf, tf, final):
    fg_ref = rest[0] if final else None
    o_ref, xn_ref, wg_buf, wu_buf, wd_buf, sem = rest[-6:]
    i = pl.program_id(0)
    F = nf * tf
    assert nf % 2 == 0

    def tile_copies(f, slot):
        col = f * tf
        return (
            pltpu.make_async_copy(wgu_hbm.at[layer, :, pl.ds(col, tf)], wg_buf.at[slot], sem.at[0, slot]),
            pltpu.make_async_copy(wgu_hbm.at[layer, :, pl.ds(F + col, tf)], wu_buf.at[slot], sem.at[1, slot]),
            pltpu.make_async_copy(wd_hbm.at[layer, pl.ds(col, tf), :], wd_buf.at[slot], sem.at[2, slot]),
        )

    def request(f, slot):
        for c in tile_copies(f, slot):
            c.start(priority=1)

    def arrive(f, slot):
        for c in tile_copies(f, slot):
            c.wait()

    def half_step(xn, slot):
        gate = _dot(xn, wg_buf[slot].astype(BF16))
        up = _dot(xn, wu_buf[slot].astype(BF16))
        act = (gate * jax.nn.sigmoid(gate) * (0.5 * up)).astype(BF16)
        return _dot(act, wd_buf[slot].astype(BF16))

    @pl.when(i == 0)
    def _():
        request(0, 0)

    arrive(0, 0)
    request(1, 1)
    h = h_ref[...]
    xn = _rms(h, g_ref[...]).astype(BF16)
    xn_ref[...] = xn
    o_ref[...] = h + half_step(xn, 0)

    def two_tiles(j, carry):
        f = 1 + 2 * j
        arrive(f, 1)
        request(f + 1, 0)
        o_ref[...] += half_step(xn_ref[...], 1)
        arrive(f + 1, 0)
        request(f + 2, 1)
        o_ref[...] += half_step(xn_ref[...], 0)
        return carry

    lax.fori_loop(0, (nf - 2) // 2, two_tiles, 0)

    arrive(nf - 1, 1)

    @pl.when(i + 1 < pl.num_programs(0))
    def _():
        request(0, 0)

    o_ref[...] += half_step(xn_ref[...], 1)
    if final:
        o_ref[...] = _rms(o_ref[...], fg_ref[...])


def _ffn(h, ln, wgu, wd, layer, *, final_gain=None, tm=1024, tf=256):
    S, D = h.shape
    F = wd.shape[1]
    nf = F // tf
    final = final_gain is not None
    in_specs = [
        pl.BlockSpec((tm, D), lambda i: (i, 0)),
        pl.BlockSpec((None, 1, D), lambda i: (layer, 0, 0)),
        pl.BlockSpec(memory_space=pl.ANY),
        pl.BlockSpec(memory_space=pl.ANY),
    ]
    args = [h, ln, wgu, wd]
    if final:
        in_specs.append(pl.BlockSpec((1, D), lambda i: (0, 0)))
        args.append(final_gain)
    return pl.pallas_call(
        functools.partial(_ffn_kernel, layer=layer, nf=nf, tf=tf, final=final),
        grid=(S // tm,),
        in_specs=in_specs,
        out_specs=pl.BlockSpec((tm, D), lambda i: (i, 0)),
        out_shape=jax.ShapeDtypeStruct((S, D), F32),
        scratch_shapes=[
            pltpu.VMEM((tm, D), BF16),
            pltpu.VMEM((2, D, tf), F32),
            pltpu.VMEM((2, D, tf), F32),
            pltpu.VMEM((2, tf, D), F32),
            pltpu.SemaphoreType.DMA((3, 2)),
        ],
        compiler_params=_params(_ARB1),
        name="ffn_final" if final else "ffn",
    )(*args)


def _qkv_kernel(h_ref, g_ref, w_ref, o_ref, xn_ref, *, n_q_blocks, scale):
    j = pl.program_id(1)
    heads, _, dh = o_ref.shape

    def project(xn):
        acc = _dot(xn, w_ref[...].astype(BF16))
        acc = (acc * jnp.where(j < n_q_blocks, scale, 1.0)).astype(BF16)
        for hd in range(heads):
            o_ref[hd] = acc[:, hd * dh:(hd + 1) * dh]

    @pl.when(j == 0)
    def _():
        xn = _rms(h_ref[...], g_ref[...]).astype(BF16)
        xn_ref[...] = xn
        project(xn)

    @pl.when(j > 0)
    def _():
        project(xn_ref[...])


def _moba_qkv(h, ln, w, layer, *, tm=1024, tn=1024):
    S, D = h.shape
    N = w.shape[2]
    dh = D // N_HEADS
    return pl.pallas_call(
        functools.partial(_qkv_kernel, n_q_blocks=D // tn, scale=LOG2E * float(dh) ** -0.5),
        grid=(S // tm, N // tn),
        in_specs=[
            pl.BlockSpec((tm, D), lambda i, j: (i, 0)),
            pl.BlockSpec((None, 1, D), lambda i, j: (layer, 0, 0)),
            pl.BlockSpec((None, D, tn), lambda i, j: (layer, 0, j)),
        ],
        out_specs=pl.BlockSpec((tn // dh, tm, dh), lambda i, j: (j, i, 0)),
        out_shape=jax.ShapeDtypeStruct((N // dh, S, dh), BF16),
        scratch_shapes=[pltpu.VMEM((tm, D), BF16)],
        compiler_params=_params(_ARB2),
        name="moba_qkv",
    )(h, ln, w)


def _tile_update(s_ref, v, m_ref, acc_ref, p_ref, r0, *, shift=None, causal=False):
    rows, tk = s_ref.shape[0] - r0, s_ref.shape[1]
    for c0 in range(r0, r0 + rows, ATTN_ROWS):
        c1 = c0 + ATTN_ROWS
        s = s_ref[c0:c1, :]
        if causal:
            qpos = lax.broadcasted_iota(jnp.int32, s.shape, 0) + (c0 - r0)
            kpos = lax.broadcasted_iota(jnp.int32, s.shape, 1)
            s = jnp.where(kpos <= qpos, s, -jnp.inf)
        m_old = m_ref[c0:c1, :]
        row_max = jnp.max(s, axis=-1, keepdims=True)
        if shift is not None:
            row_max = row_max - shift
        m_new = jnp.maximum(m_old, row_max)
        m_ref[c0:c1, :] = m_new
        sub = m_new if shift is None else m_new + shift
        p_ref[c0:c1, :] = jnp.exp2(s - jnp.tile(sub, (1, tk // LANES))).astype(BF16)
        alpha = jnp.exp2(m_old - m_new)
        acc_ref[c0:c1, :] = jnp.tile(alpha, (1, acc_ref.shape[1] // LANES)) * acc_ref[c0:c1, :]
    acc_ref[r0:, :] += _dot(p_ref[r0:, :], v)


def _causal_attention(qi, score_fn, va_ref, s_refs, m_ref, acc_ref, p_ref, o_ref,
                      *, tq, tk, slope2=None):
    per = tq // tk
    assert per % 2 == 0
    dv = o_ref.shape[1]
    m_ref[...] = jnp.full_like(m_ref, M_INIT)
    acc_ref[...] = jnp.zeros_like(acc_ref)
    s_refs[0][...] = score_fn(0, 0)

    def tile(t, rd, wr):
        start = pl.multiple_of(t * tk, tk)
        wr[...] = score_fn(0, start + tk)
        shift = None if slope2 is None else slope2 * (qi * tq - t * tk).astype(F32)
        _tile_update(rd, va_ref[pl.ds(start, tk), :], m_ref, acc_ref, p_ref, 0, shift=shift)

    def pair(j):
        tile(2 * j, s_refs[0], s_refs[1])
        tile(2 * j + 1, s_refs[1], s_refs[0])

    def quad(j, carry):
        pair(2 * j)
        pair(2 * j + 1)
        return carry

    n_pairs = qi * (per // 2)
    lax.fori_loop(0, n_pairs // 2, quad, 0)

    @pl.when(n_pairs % 2 == 1)
    def _():
        pair(n_pairs - 1)

    for b in range(per):
        r0 = b * tk
        start = pl.multiple_of(qi * tq + r0, tk)
        rd, wr = s_refs[b % 2], s_refs[1 - b % 2]
        if b + 1 < per:
            wr[r0 + tk:, :] = score_fn(r0 + tk, start + tk)
        shift = None if slope2 is None else slope2 * float(-r0)
        _tile_update(rd, va_ref[pl.ds(start, tk), :], m_ref, acc_ref, p_ref, r0,
                     shift=shift, causal=True)

    o_ref[...] = (acc_ref[:, :dv] / acc_ref[:, dv:]).astype(BF16)


def _attn_scratch(S, tq, tk, dv):
    return [
        pltpu.VMEM((S, dv + LANES), BF16),
        pltpu.VMEM((tq, tk), F32),
        pltpu.VMEM((tq, tk), F32),
        pltpu.VMEM((tq, LANES), F32),
        pltpu.VMEM((tq, dv + LANES), F32),
        pltpu.VMEM((tq, tk), BF16),
    ]


def _fill_values(va_ref, v_ref):
    dv = v_ref.shape[1]
    va_ref[:, :dv] = v_ref[...]
    va_ref[:, dv:] = jnp.ones((va_ref.shape[0], va_ref.shape[1] - dv), BF16)


def _split3(x):
    hi = x.astype(BF16).astype(F32)
    mid = (x - hi).astype(BF16).astype(F32)
    lo = ((x - hi) - mid).astype(BF16).astype(F32)
    return hi, mid, lo


def _moba_kernel(slopes_ref, q_ref, k_ref, v_ref, o_ref,
                 kaug_ref, kmean_ref, qaug_ref, va_ref, s0_ref, s1_ref, m_ref, acc_ref, p_ref,
                 *, nb, bs, topk, tq, tk):
    h = pl.program_id(0)
    qi = pl.program_id(1)
    slope2 = slopes_ref[h] * LOG2E
    dh = q_ref.shape[1]
    nbp = -(-nb // 8) * 8
    assert nbp + 8 <= LANES

    @pl.when(qi == 0)
    def _():
        _fill_values(va_ref, v_ref)
        kaug_ref[:, :dh] = k_ref[...]
        lane_b = lax.broadcasted_iota(jnp.int32, (bs, LANES), 1)
        row_b = lax.broadcasted_iota(jnp.int32, (bs, LANES), 0)
        kmean_ref[...] = jnp.zeros_like(kmean_ref)
        for n in range(nb):
            hi, mid, lo = _split3(slope2 * (row_b + (n * bs) % tk).astype(F32))
            extra = jnp.where(lane_b == n, 1.0, 0.0)
            extra = jnp.where(lane_b == nbp, hi, extra)
            extra = jnp.where(lane_b == nbp + 1, mid, extra)
            extra = jnp.where(lane_b == nbp + 2, lo, extra)
            kaug_ref[n * bs:(n + 1) * bs, dh:] = extra.astype(BF16)
            kmean_ref[n:n + 1, :] = jnp.mean(
                k_ref[n * bs:(n + 1) * bs, :].astype(F32), axis=0, keepdims=True)

    q = q_ref[...]

    gate = _dot_nt(kmean_ref[...].astype(BF16), q)[:nbp, :]
    blk = lax.broadcasted_iota(jnp.int32, (nbp, tq), 0)
    cur = qi * (tq // bs) + lax.broadcasted_iota(jnp.int32, (nbp, tq), 1) // bs
    valid = blk < cur
    blk_f = blk.astype(F32)
    g = jnp.where(valid, gate, -jnp.inf)
    picked = jnp.zeros((nbp, tq), jnp.bool_)
    for _ in range(topk):
        mx = jnp.max(g, axis=0, keepdims=True)
        first = jnp.min(jnp.where(g == mx, blk_f, float(nbp)), axis=0, keepdims=True)
        hit = blk_f == first
        picked = jnp.logical_or(picked, hit)
        g = jnp.where(hit, -jnp.inf, g)
    attend = jnp.logical_or(jnp.logical_and(picked, valid), blk == cur)
    extra_t = jnp.concatenate(
        [jnp.where(attend, 0.0, MASK_BIAS),
         jnp.ones((8, tq), F32),
         jnp.zeros((LANES - nbp - 8, tq), F32)], axis=0)
    qaug_ref[...] = jnp.concatenate([q, extra_t.T.astype(BF16)], axis=1)

    def score_fn(r0, start):
        return _dot_nt(qaug_ref[r0:, :], kaug_ref[pl.ds(start, tk), :])

    _causal_attention(qi, score_fn, va_ref, (s0_ref, s1_ref), m_ref, acc_ref, p_ref, o_ref,
                      tq=tq, tk=tk, slope2=slope2)


def _moba_attention(qkv, slopes, *, tq=ATTN_TQ, tk=ATTN_TK):
    H = N_HEADS
    _, S, dh = qkv.shape
    bs = MOBA_BLOCK
    nb = S // bs
    assert S % tq == 0 and tq % tk == 0 and tk % bs == 0 and nb <= LANES and dh == LANES
    return pl.pallas_call(
        functools.partial(_moba_kernel, nb=nb, bs=bs, topk=min(MOBA_TOPK, nb), tq=tq, tk=tk),
        grid=(H, S // tq),
        in_specs=[
            pl.BlockSpec(memory_space=pltpu.SMEM),
            pl.BlockSpec((None, tq, dh), lambda h, i: (h, i, 0)),
            pl.BlockSpec((None, S, dh), lambda h, i: (H + h, 0, 0)),
            pl.BlockSpec((None, S, dh), lambda h, i: (2 * H + h, 0, 0)),
        ],
        out_specs=pl.BlockSpec((tq, dh), lambda h, i: (i, h)),
        out_shape=jax.ShapeDtypeStruct((S, H * dh), BF16),
        scratch_shapes=[
            pltpu.VMEM((S, dh + LANES), BF16),
            pltpu.VMEM((LANES, dh), F32),
            pltpu.VMEM((tq, dh + LANES), BF16),
        ] + _attn_scratch(S, tq, tk, dh),
        compiler_params=_params(_ARB2),
        name="moba_attn",
    )(slopes, qkv, qkv, qkv)


def _oproj_kernel(h_ref, o_ref, w_ref, out_ref):
    out_ref[...] = h_ref[...] + _dot(o_ref[...], w_ref[...])


def _oproj(h, o, w, layer, *, tm=512):
    S, D = h.shape
    K = o.shape[1]
    return pl.pallas_call(
        _oproj_kernel,
        grid=(S // tm,),
        in_specs=[
            pl.BlockSpec((tm, D), lambda i: (i, 0)),
            pl.BlockSpec((tm, K), lambda i: (i, 0)),
            pl.BlockSpec((None, K, D), lambda i: (layer, 0, 0)),
        ],
        out_specs=pl.BlockSpec((tm, D), lambda i: (i, 0)),
        out_shape=jax.ShapeDtypeStruct((S, D), F32),
        compiler_params=_params(_ARB1),
        name="oproj",
    )(h, o, w)


def _rope_pair(pair, cs):
    y = pair * cs
    z = y + pltpu.roll(y, LANES // 2, 1)
    lane = lax.broadcasted_iota(jnp.int32, y.shape, 1)
    return jnp.where(lane < LANES // 2, z, 0.0)


def _mla_kv_kernel(h_ref, g_ref, wd_ref, cg_ref, wu_ref, cs_ref, k_ref, v_ref, *, nh):
    hn = _rms(h_ref[...], g_ref[...]).astype(BF16)
    t = _dot(hn, wd_ref[...])
    c = _rms(t[:, :KV_LORA], cg_ref[...]).astype(BF16)
    k_rope = _rope_pair(t[:, KV_LORA:], cs_ref[...])
    for hd in range(nh):
        a = _dot(c, wu_ref[:, hd * (QK_NOPE + V_DIM):(hd + 1) * (QK_NOPE + V_DIM)])
        k_ref[hd] = jnp.concatenate([a[:, :QK_NOPE], k_rope], axis=1).astype(BF16)
        v_ref[hd] = a[:, QK_NOPE:].astype(BF16)


def _mla_kv(h, g, wd, cg, wu, cs, *, tm=512):
    S, D = h.shape
    H = N_HEADS
    return pl.pallas_call(
        functools.partial(_mla_kv_kernel, nh=H),
        grid=(S // tm,),
        in_specs=[
            pl.BlockSpec((tm, D), lambda i: (i, 0)),
            pl.BlockSpec((1, D), lambda i: (0, 0)),
            pl.BlockSpec(wd.shape, lambda i: (0, 0)),
            pl.BlockSpec((1, KV_LORA), lambda i: (0, 0)),
            pl.BlockSpec(wu.shape, lambda i: (0, 0)),
            pl.BlockSpec((tm, LANES), lambda i: (i, 0)),
        ],
        out_specs=[
            pl.BlockSpec((H, tm, HEAD_PAD), lambda i: (0, i, 0)),
            pl.BlockSpec((H, tm, V_DIM), lambda i: (0, i, 0)),
        ],
        out_shape=[
            jax.ShapeDtypeStruct((H, S, HEAD_PAD), BF16),
            jax.ShapeDtypeStruct((H, S, V_DIM), BF16),
        ],
        compiler_params=_params(_ARB1),
        name="mla_kv",
    )(h, g, wd, cg, wu, cs)


def _mla_q_kernel(h_ref, g_ref, wd_ref, cg_ref, wu_ref, cs_ref, q_ref, *, nh, scale):
    xn = _rms(h_ref[...], g_ref[...]).astype(BF16)
    cq = _rms(_dot(xn, wd_ref[...]), cg_ref[...]).astype(BF16)
    cs = cs_ref[...]
    for hd in range(nh):
        a = _dot(cq, wu_ref[:, hd * HEAD_PAD:(hd + 1) * HEAD_PAD])
        q_rope = _rope_pair(a[:, QK_NOPE:], cs)
        q_ref[hd] = (jnp.concatenate([a[:, :QK_NOPE], q_rope], axis=1) * scale).astype(BF16)


def _mla_q(h, ln, wd, cg, wu, cs, layer, j, *, tm=512):
    S, D = h.shape
    H = N_HEADS
    R = wd.shape[2]
    return pl.pallas_call(
        functools.partial(_mla_q_kernel, nh=H, scale=LOG2E * float(QK_NOPE + QK_ROPE) ** -0.5),
        grid=(S // tm,),
        in_specs=[
            pl.BlockSpec((tm, D), lambda i: (i, 0)),
            pl.BlockSpec((None, 1, D), lambda i: (layer, 0, 0)),
            pl.BlockSpec((None, D, R), lambda i: (j, 0, 0)),
            pl.BlockSpec((None, 1, R), lambda i: (j, 0, 0)),
            pl.BlockSpec((None, R, H * HEAD_PAD), lambda i: (j, 0, 0)),
            pl.BlockSpec((tm, LANES), lambda i: (i, 0)),
        ],
        out_specs=pl.BlockSpec((H, tm, HEAD_PAD), lambda i: (0, i, 0)),
        out_shape=jax.ShapeDtypeStruct((H, S, HEAD_PAD), BF16),
        compiler_params=_params(_ARB1),
        name="mla_q",
    )(h, ln, wd, cg, wu, cs)


def _mla_attn_kernel(q_ref, k_ref, v_ref, o_ref, va_ref, s0_ref, s1_ref, m_ref, acc_ref, p_ref,
                     *, tq, tk):
    qi = pl.program_id(1)

    @pl.when(qi == 0)
    def _():
        _fill_values(va_ref, v_ref)

    def score_fn(r0, start):
        return _dot_nt(q_ref[r0:, :], k_ref[pl.ds(start, tk), :])

    _causal_attention(qi, score_fn, va_ref, (s0_ref, s1_ref), m_ref, acc_ref, p_ref, o_ref,
                      tq=tq, tk=tk)


def _mla_attention(q, k, v, *, tq=ATTN_TQ, tk=ATTN_TK):
    H, S, _ = q.shape
    assert S % tq == 0 and tq % tk == 0
    return pl.pallas_call(
        functools.partial(_mla_attn_kernel, tq=tq, tk=tk),
        grid=(H, S // tq),
        in_specs=[
            pl.BlockSpec((None, tq, HEAD_PAD), lambda h, i: (h, i, 0)),
            pl.BlockSpec((None, S, HEAD_PAD), lambda h, i: (h, 0, 0)),
            pl.BlockSpec((None, S, V_DIM), lambda h, i: (h, 0, 0)),
        ],
        out_specs=pl.BlockSpec((tq, V_DIM), lambda h, i: (i, h)),
        out_shape=jax.ShapeDtypeStruct((S, H * V_DIM), BF16),
        scratch_shapes=_attn_scratch(S, tq, tk, V_DIM),
        compiler_params=_params(_ARB2),
        name="mla_attn",
    )(q, k, v)


def _rotate_half_cols(w):
    w1, w2 = jnp.split(w, 2, axis=-1)
    return jnp.concatenate([-w2, w1], axis=-1)


def _alibi_slopes(n):
    return jnp.asarray(np.array([2.0 ** (-8.0 * (i + 1) / n) for i in range(n)], dtype=np.float32))


def _rope_table(s):
    inv = 1.0 / (ROPE_THETA ** (jnp.arange(0, QK_ROPE, 2, dtype=F32) / QK_ROPE))
    ang = jnp.arange(s, dtype=F32)[:, None] * inv[None, :]
    cos, sin = jnp.cos(ang), jnp.sin(ang)
    return jnp.concatenate([cos, cos, sin, sin], axis=-1)


def kernel(x, ln_ffn1, ffn1_wgu, ffn1_wd, ln_mix, ln_ffn2, ffn2_wgu, ffn2_wd, moba_wqkv, moba_wo, kv_norm, mla_wdkv, ckv_norm, mla_wukv, mla_wdq, cq_norm, mla_wuq, mla_wo, final_norm):
    B, S, D = x.shape
    assert B == 1
    depth = ln_ffn1.shape[0]
    n_a = moba_wqkv.shape[0]
    n_b = mla_wdq.shape[0]
    H = N_HEADS

    ln1 = ln_ffn1.reshape(depth, 1, D)
    lnm = ln_mix.reshape(depth, 1, D)
    ln2 = ln_ffn2.reshape(depth, 1, D)
    wqkv, wo_a = moba_wqkv, moba_wo.astype(BF16)
    wo_b = mla_wo.astype(BF16)
    slopes = _alibi_slopes(H)
    cs = _rope_table(S)

    kr = mla_wdkv[:, KV_LORA:]
    wdkv = jnp.concatenate([mla_wdkv, _rotate_half_cols(kr)], axis=-1).astype(BF16)
    wukv = mla_wukv.astype(BF16)
    wdq = mla_wdq.astype(BF16)
    R = mla_wuq.shape[1]
    wuq4 = mla_wuq.reshape(n_b, R, H, QK_NOPE + QK_ROPE)
    rope_cols = wuq4[..., QK_NOPE:]
    wuq = jnp.concatenate([wuq4, _rotate_half_cols(rope_cols)], axis=-1)
    wuq = wuq.reshape(n_b, R, H * HEAD_PAD).astype(BF16)
    cqn = cq_norm.reshape(n_b, 1, R)

    h = x.reshape(S, D)

    fg = final_norm.reshape(1, D)

    def macaron(h, l, mixer):
        h = _ffn(h, ln1, ffn1_wgu, ffn1_wd, l)
        h = mixer(h)
        return _ffn(h, ln2, ffn2_wgu, ffn2_wd, l, final_gain=fg if l == depth - 1 else None)

    for l in range(n_a):
        def moba(h, l=l):
            qkv = _moba_qkv(h, lnm, wqkv, l)
            o = _moba_attention(qkv, slopes)
            return _oproj(h, o, wo_a, l)
        h = macaron(h, l, moba)

    k_sh, v_sh = _mla_kv(h, kv_norm.reshape(1, D), wdkv, ckv_norm.reshape(1, KV_LORA), wukv, cs)

    for j in range(n_b):
        l = n_a + j

        def mla(h, l=l, j=j):
            q = _mla_q(h, lnm, wdq, cqn, wuq, cs, l, j)
            o = _mla_attention(q, k_sh, v_sh)
            return _oproj(h, o, wo_b, j)
        h = macaron(h, l, mla)

    return h.reshape(B, S, D)
```

```python
import functools

import numpy as np
import jax
import jax.numpy as jnp
from jax import lax
from jax.experimental import pallas as pl
from jax.experimental.pallas import tpu as pltpu

F32 = jnp.float32
BF16 = jnp.bfloat16

EPS = 1e-6
ROPE_THETA = 10000.0

N_HEADS = 16
MOBA_BLOCK = 256
MOBA_TOPK = 3
KV_LORA = 512
QK_NOPE = 128
QK_ROPE = 64
V_DIM = 128

LANES = 128
HEAD_PAD = 2 * LANES
MASK_BIAS = -1e9
M_INIT = -1e30
LOG2E = 1.4426950408889634
VMEM_LIMIT = 56 * 1024 * 1024

ATTN_TQ = 1024
ATTN_TK = 512
ATTN_ROWS = 256

_ARB1 = ("arbitrary",)
_ARB2 = ("arbitrary", "arbitrary")


def _params(sem):
    return pltpu.CompilerParams(dimension_semantics=sem, vmem_limit_bytes=VMEM_LIMIT)


def _rms(x, g):
    ms = jnp.mean(x * x, axis=-1, keepdims=True)
    return x * lax.rsqrt(ms + EPS) * g


def _dot(a, b):
    return jnp.dot(a, b, preferred_element_type=F32)


def _dot_nt(a, b):
    return lax.dot_general(a, b, (((1,), (1,)), ((), ())), preferred_element_type=F32)


def _ffn_kernel(h_ref, g_ref, wgu_hbm, wd_hbm, *rest, layer, nf, tf, final):
    fg_ref = rest[0] if final else None
    o_ref, xn_ref, wg_buf, wu_buf, wd_buf, sem = rest[-6:]
    i = pl.program_id(0)
    F = nf * tf
    assert nf % 2 == 0

    def tile_copies(f, slot):
        col = f * tf
        return (
            pltpu.make_async_copy(wgu_hbm.at[layer, :, pl.ds(col, tf)], wg_buf.at[slot], sem.at[0, slot]),
            pltpu.make_async_copy(wgu_hbm.at[layer, :, pl.ds(F + col, tf)], wu_buf.at[slot], sem.at[1, slot]),
            pltpu.make_async_copy(wd_hbm.at[layer, pl.ds(col, tf), :], wd_buf.at[slot], sem.at[2, slot]),
        )

    def request(f, slot):
        for c in tile_copies(f, slot):
            c.start()

    def arrive(f, slot):
        for c in tile_copies(f, slot):
            c.wait()

    def half_step(xn, slot):
        gate = _dot(xn, wg_buf[slot].astype(BF16))
        up = _dot(xn, wu_buf[slot].astype(BF16))
        act = (gate * jax.nn.sigmoid(gate) * (0.5 * up)).astype(BF16)
        return _dot(act, wd_buf[slot].astype(BF16))

    @pl.when(i == 0)
    def _():
        request(0, 0)

    request(1, 1)
    arrive(0, 0)
    h = h_ref[...]
    xn = _rms(h, g_ref[...]).astype(BF16)
    xn_ref[...] = xn
    o_ref[...] = h + half_step(xn, 0)

    def two_tiles(j, carry):
        f = 1 + 2 * j
        request(f + 1, 0)
        arrive(f, 1)
        o_ref[...] += half_step(xn_ref[...], 1)
        request(f + 2, 1)
        arrive(f + 1, 0)
        o_ref[...] += half_step(xn_ref[...], 0)
        return carry

    lax.fori_loop(0, (nf - 2) // 2, two_tiles, 0)

    @pl.when(i + 1 < pl.num_programs(0))
    def _():
        request(0, 0)

    arrive(nf - 1, 1)
    o_ref[...] += half_step(xn_ref[...], 1)
    if final:
        o_ref[...] = _rms(o_ref[...], fg_ref[...])


def _ffn(h, ln, wgu, wd, layer, *, final_gain=None, tm=1024, tf=256):
    S, D = h.shape
    F = wd.shape[1]
    nf = F // tf
    final = final_gain is not None
    in_specs = [
        pl.BlockSpec((tm, D), lambda i: (i, 0)),
        pl.BlockSpec((None, 1, D), lambda i: (layer, 0, 0)),
        pl.BlockSpec(memory_space=pl.ANY),
        pl.BlockSpec(memory_space=pl.ANY),
    ]
    args = [h, ln, wgu, wd]
    if final:
        in_specs.append(pl.BlockSpec((1, D), lambda i: (0, 0)))
        args.append(final_gain)
    return pl.pallas_call(
        functools.partial(_ffn_kernel, layer=layer, nf=nf, tf=tf, final=final),
        grid=(S // tm,),
        in_specs=in_specs,
        out_specs=pl.BlockSpec((tm, D), lambda i: (i, 0)),
        out_shape=jax.ShapeDtypeStruct((S, D), F32),
        scratch_shapes=[
            pltpu.VMEM((tm, D), BF16),
            pltpu.VMEM((2, D, tf), F32),
            pltpu.VMEM((2, D, tf), F32),
            pltpu.VMEM((2, tf, D), F32),
            pltpu.SemaphoreType.DMA((3, 2)),
        ],
        compiler_params=_params(_ARB1),
        name="ffn_final" if final else "ffn",
    )(*args)


def _qkv_kernel(h_ref, g_ref, w_ref, o_ref, xn_ref, *, n_q_blocks, scale):
    j = pl.program_id(1)
    heads, _, dh = o_ref.shape

    def project(xn):
        acc = _dot(xn, w_ref[...].astype(BF16))
        acc = (acc * jnp.where(j < n_q_blocks, scale, 1.0)).astype(BF16)
        for hd in range(heads):
            o_ref[hd] = acc[:, hd * dh:(hd + 1) * dh]

    @pl.when(j == 0)
    def _():
        xn = _rms(h_ref[...], g_ref[...]).astype(BF16)
        xn_ref[...] = xn
        project(xn)

    @pl.when(j > 0)
    def _():
        project(xn_ref[...])


def _moba_qkv(h, ln, w, layer, *, tm=1024, tn=1024):
    S, D = h.shape
    N = w.shape[2]
    dh = D // N_HEADS
    return pl.pallas_call(
        functools.partial(_qkv_kernel, n_q_blocks=D // tn, scale=LOG2E * float(dh) ** -0.5),
        grid=(S // tm, N // tn),
        in_specs=[
            pl.BlockSpec((tm, D), lambda i, j: (i, 0)),
            pl.BlockSpec((None, 1, D), lambda i, j: (layer, 0, 0)),
            pl.BlockSpec((None, D, tn), lambda i, j: (layer, 0, j)),
        ],
        out_specs=pl.BlockSpec((tn // dh, tm, dh), lambda i, j: (j, i, 0)),
        out_shape=jax.ShapeDtypeStruct((N // dh, S, dh), BF16),
        scratch_shapes=[pltpu.VMEM((tm, D), BF16)],
        compiler_params=_params(_ARB2),
        name="moba_qkv",
    )(h, ln, w)


def _tile_update(s_ref, v, m_ref, acc_ref, p_ref, r0, *, shift=None, causal=False):
    rows, tk = s_ref.shape[0] - r0, s_ref.shape[1]
    for c0 in range(r0, r0 + rows, ATTN_ROWS):
        c1 = c0 + ATTN_ROWS
        s = s_ref[c0:c1, :]
        if causal:
            qpos = lax.broadcasted_iota(jnp.int32, s.shape, 0) + (c0 - r0)
            kpos = lax.broadcasted_iota(jnp.int32, s.shape, 1)
            s = jnp.where(kpos <= qpos, s, -jnp.inf)
        m_old = m_ref[c0:c1, :]
        row_max = jnp.max(s, axis=-1, keepdims=True)
        if shift is not None:
            row_max = row_max - shift
        m_new = jnp.maximum(m_old, row_max)
        m_ref[c0:c1, :] = m_new
        sub = m_new if shift is None else m_new + shift
        p_ref[c0:c1, :] = jnp.exp2(s - jnp.tile(sub, (1, tk // LANES))).astype(BF16)
        alpha = jnp.exp2(m_old - m_new)
        acc_ref[c0:c1, :] = jnp.tile(alpha, (1, acc_ref.shape[1] // LANES)) * acc_ref[c0:c1, :]
    acc_ref[r0:, :] += _dot(p_ref[r0:, :], v)


def _causal_attention(qi, score_fn, va_ref, s_refs, m_ref, acc_ref, p_ref, o_ref,
                      *, tq, tk, slope2=None):
    per = tq // tk
    assert per % 2 == 0
    dv = o_ref.shape[1]
    m_ref[...] = jnp.full_like(m_ref, M_INIT)
    acc_ref[...] = jnp.zeros_like(acc_ref)
    s_refs[0][...] = score_fn(0, 0)

    def tile(t, rd, wr):
        start = pl.multiple_of(t * tk, tk)
        wr[...] = score_fn(0, start + tk)
        shift = None if slope2 is None else slope2 * (qi * tq - t * tk).astype(F32)
        _tile_update(rd, va_ref[pl.ds(start, tk), :], m_ref, acc_ref, p_ref, 0, shift=shift)

    def pair(j):
        tile(2 * j, s_refs[0], s_refs[1])
        tile(2 * j + 1, s_refs[1], s_refs[0])

    def quad(j, carry):
        pair(2 * j)
        pair(2 * j + 1)
        return carry

    n_pairs = qi * (per // 2)
    lax.fori_loop(0, n_pairs // 2, quad, 0)

    @pl.when(n_pairs % 2 == 1)
    def _():
        pair(n_pairs - 1)

    for b in range(per):
        r0 = b * tk
        start = pl.multiple_of(qi * tq + r0, tk)
        rd, wr = s_refs[b % 2], s_refs[1 - b % 2]
        if b + 1 < per:
            wr[r0 + tk:, :] = score_fn(r0 + tk, start + tk)
        shift = None if slope2 is None else slope2 * float(-r0)
        _tile_update(rd, va_ref[pl.ds(start, tk), :], m_ref, acc_ref, p_ref, r0,
                     shift=shift, causal=True)

    o_ref[...] = (acc_ref[:, :dv] / acc_ref[:, dv:]).astype(BF16)


def _attn_scratch(S, tq, tk, dv):
    return [
        pltpu.VMEM((S, dv + LANES), BF16),
        pltpu.VMEM((tq, tk), F32),
        pltpu.VMEM((tq, tk), F32),
        pltpu.VMEM((tq, LANES), F32),
        pltpu.VMEM((tq, dv + LANES), F32),
        pltpu.VMEM((tq, tk), BF16),
    ]


def _fill_values(va_ref, v_ref):
    dv = v_ref.shape[1]
    va_ref[:, :dv] = v_ref[...]
    va_ref[:, dv:] = jnp.ones((va_ref.shape[0], va_ref.shape[1] - dv), BF16)


def _split3(x):
    hi = x.astype(BF16).astype(F32)
    mid = (x - hi).astype(BF16).astype(F32)
    lo = ((x - hi) - mid).astype(BF16).astype(F32)
    return hi, mid, lo


def _moba_kernel(slopes_ref, q_ref, k_ref, v_ref, o_ref,
                 kaug_ref, kmean_ref, qaug_ref, va_ref, s0_ref, s1_ref, m_ref, acc_ref, p_ref,
                 *, nb, bs, topk, tq, tk):
    h = pl.program_id(0)
    qi = pl.program_id(1)
    slope2 = slopes_ref[h] * LOG2E
    dh = q_ref.shape[1]
    nbp = -(-nb // 8) * 8
    assert nbp + 8 <= LANES

    @pl.when(qi == 0)
    def _():
        _fill_values(va_ref, v_ref)
        kaug_ref[:, :dh] = k_ref[...]
        lane_b = lax.broadcasted_iota(jnp.int32, (bs, LANES), 1)
        row_b = lax.broadcasted_iota(jnp.int32, (bs, LANES), 0)
        kmean_ref[...] = jnp.zeros_like(kmean_ref)
        for n in range(nb):
            hi, mid, lo = _split3(slope2 * (row_b + (n * bs) % tk).astype(F32))
            extra = jnp.where(lane_b == n, 1.0, 0.0)
            extra = jnp.where(lane_b == nbp, hi, extra)
            extra = jnp.where(lane_b == nbp + 1, mid, extra)
            extra = jnp.where(lane_b == nbp + 2, lo, extra)
            kaug_ref[n * bs:(n + 1) * bs, dh:] = extra.astype(BF16)
            kmean_ref[n:n + 1, :] = jnp.mean(
                k_ref[n * bs:(n + 1) * bs, :].astype(F32), axis=0, keepdims=True)

    q = q_ref[...]

    gate = _dot_nt(kmean_ref[...].astype(BF16), q)[:nbp, :]
    blk = lax.broadcasted_iota(jnp.int32, (nbp, tq), 0)
    cur = qi * (tq // bs) + lax.broadcasted_iota(jnp.int32, (nbp, tq), 1) // bs
    valid = blk < cur
    blk_f = blk.astype(F32)
    g = jnp.where(valid, gate, -jnp.inf)
    picked = jnp.zeros((nbp, tq), jnp.bool_)
    for _ in range(topk):
        mx = jnp.max(g, axis=0, keepdims=True)
        first = jnp.min(jnp.where(g == mx, blk_f, float(nbp)), axis=0, keepdims=True)
        hit = blk_f == first
        picked = jnp.logical_or(picked, hit)
        g = jnp.where(hit, -jnp.inf, g)
    attend = jnp.logical_or(jnp.logical_and(picked, valid), blk == cur)
    extra_t = jnp.concatenate(
        [jnp.where(attend, 0.0, MASK_BIAS),
         jnp.ones((8, tq), F32),
         jnp.zeros((LANES - nbp - 8, tq), F32)], axis=0)
    qaug_ref[...] = jnp.concatenate([q, extra_t.T.astype(BF16)], axis=1)

    def score_fn(r0, start):
        return _dot_nt(qaug_ref[r0:, :], kaug_ref[pl.ds(start, tk), :])

    _causal_attention(qi, score_fn, va_ref, (s0_ref, s1_ref), m_ref, acc_ref, p_ref, o_ref,
                      tq=tq, tk=tk, slope2=slope2)


def _moba_attention(qkv, slopes, *, tq=ATTN_TQ, tk=ATTN_TK):
    H = N_HEADS
    _, S, dh = qkv.shape
    bs = MOBA_BLOCK
    nb = S // bs
    assert S % tq == 0 and tq % tk == 0 and tk % bs == 0 and nb <= LANES and dh == LANES
    return pl.pallas_call(
        functools.partial(_moba_kernel, nb=nb, bs=bs, topk=min(MOBA_TOPK, nb), tq=tq, tk=tk),
        grid=(H, S // tq),
        in_specs=[
            pl.BlockSpec(memory_space=pltpu.SMEM),
            pl.BlockSpec((None, tq, dh), lambda h, i: (h, i, 0)),
            pl.BlockSpec((None, S, dh), lambda h, i: (H + h, 0, 0)),
            pl.BlockSpec((None, S, dh), lambda h, i: (2 * H + h, 0, 0)),
        ],
        out_specs=pl.BlockSpec((tq, dh), lambda h, i: (i, h)),
        out_shape=jax.ShapeDtypeStruct((S, H * dh), BF16),
        scratch_shapes=[
            pltpu.VMEM((S, dh + LANES), BF16),
            pltpu.VMEM((LANES, dh), F32),
            pltpu.VMEM((tq, dh + LANES), BF16),
        ] + _attn_scratch(S, tq, tk, dh),
        compiler_params=_params(_ARB2),
        name="moba_attn",
    )(slopes, qkv, qkv, qkv)


def _oproj_kernel(h_ref, o_ref, w_ref, out_ref):
    out_ref[...] = h_ref[...] + _dot(o_ref[...], w_ref[...])


def _oproj(h, o, w, layer, *, tm=512):
    S, D = h.shape
    K = o.shape[1]
    return pl.pallas_call(
        _oproj_kernel,
        grid=(S // tm,),
        in_specs=[
            pl.BlockSpec((tm, D), lambda i: (i, 0)),
            pl.BlockSpec((tm, K), lambda i: (i, 0)),
            pl.BlockSpec((None, K, D), lambda i: (layer, 0, 0)),
        ],
        out_specs=pl.BlockSpec((tm, D), lambda i: (i, 0)),
        out_shape=jax.ShapeDtypeStruct((S, D), F32),
        compiler_params=_params(_ARB1),
        name="oproj",
    )(h, o, w)


def _rope_pair(pair, cs):
    y = pair * cs
    z = y + pltpu.roll(y, LANES // 2, 1)
    lane = lax.broadcasted_iota(jnp.int32, y.shape, 1)
    return jnp.where(lane < LANES // 2, z, 0.0)


def _mla_kv_kernel(h_ref, g_ref, wd_ref, cg_ref, wu_ref, cs_ref, k_ref, v_ref, *, nh):
    hn = _rms(h_ref[...], g_ref[...]).astype(BF16)
    t = _dot(hn, wd_ref[...])
    c = _rms(t[:, :KV_LORA], cg_ref[...]).astype(BF16)
    k_rope = _rope_pair(t[:, KV_LORA:], cs_ref[...])
    for hd in range(nh):
        a = _dot(c, wu_ref[:, hd * (QK_NOPE + V_DIM):(hd + 1) * (QK_NOPE + V_DIM)])
        k_ref[hd] = jnp.concatenate([a[:, :QK_NOPE], k_rope], axis=1).astype(BF16)
        v_ref[hd] = a[:, QK_NOPE:].astype(BF16)


def _mla_kv(h, g, wd, cg, wu, cs, *, tm=512):
    S, D = h.shape
    H = N_HEADS
    return pl.pallas_call(
        functools.partial(_mla_kv_kernel, nh=H),
        grid=(S // tm,),
        in_specs=[
            pl.BlockSpec((tm, D), lambda i: (i, 0)),
            pl.BlockSpec((1, D), lambda i: (0, 0)),
            pl.BlockSpec(wd.shape, lambda i: (0, 0)),
            pl.BlockSpec((1, KV_LORA), lambda i: (0, 0)),
            pl.BlockSpec(wu.shape, lambda i: (0, 0)),
            pl.BlockSpec((tm, LANES), lambda i: (i, 0)),
        ],
        out_specs=[
            pl.BlockSpec((H, tm, HEAD_PAD), lambda i: (0, i, 0)),
            pl.BlockSpec((H, tm, V_DIM), lambda i: (0, i, 0)),
        ],
        out_shape=[
            jax.ShapeDtypeStruct((H, S, HEAD_PAD), BF16),
            jax.ShapeDtypeStruct((H, S, V_DIM), BF16),
        ],
        compiler_params=_params(_ARB1),
        name="mla_kv",
    )(h, g, wd, cg, wu, cs)


def _mla_q_kernel(h_ref, g_ref, wd_ref, cg_ref, wu_ref, cs_ref, q_ref, *, nh, scale):
    xn = _rms(h_ref[...], g_ref[...]).astype(BF16)
    cq = _rms(_dot(xn, wd_ref[...]), cg_ref[...]).astype(BF16)
    cs = cs_ref[...]
    for hd in range(nh):
        a = _dot(cq, wu_ref[:, hd * HEAD_PAD:(hd + 1) * HEAD_PAD])
        q_rope = _rope_pair(a[:, QK_NOPE:], cs)
        q_ref[hd] = (jnp.concatenate([a[:, :QK_NOPE], q_rope], axis=1) * scale).astype(BF16)


def _mla_q(h, ln, wd, cg, wu, cs, layer, j, *, tm=512):
    S, D = h.shape
    H = N_HEADS
    R = wd.shape[2]
    return pl.pallas_call(
        functools.partial(_mla_q_kernel, nh=H, scale=LOG2E * float(QK_NOPE + QK_ROPE) ** -0.5),
        grid=(S // tm,),
        in_specs=[
            pl.BlockSpec((tm, D), lambda i: (i, 0)),
            pl.BlockSpec((None, 1, D), lambda i: (layer, 0, 0)),
            pl.BlockSpec((None, D, R), lambda i: (j, 0, 0)),
            pl.BlockSpec((None, 1, R), lambda i: (j, 0, 0)),
            pl.BlockSpec((None, R, H * HEAD_PAD), lambda i: (j, 0, 0)),
            pl.BlockSpec((tm, LANES), lambda i: (i, 0)),
        ],
        out_specs=pl.BlockSpec((H, tm, HEAD_PAD), lambda i: (0, i, 0)),
        out_shape=jax.ShapeDtypeStruct((H, S, HEAD_PAD), BF16),
        compiler_params=_params(_ARB1),
        name="mla_q",
    )(h, ln, wd, cg, wu, cs)


def _mla_attn_kernel(q_ref, k_ref, v_ref, o_ref, va_ref, s0_ref, s1_ref, m_ref, acc_ref, p_ref,
                     *, tq, tk):
    qi = pl.program_id(1)

    @pl.when(qi == 0)
    def _():
        _fill_values(va_ref, v_ref)

    def score_fn(r0, start):
        return _dot_nt(q_ref[r0:, :], k_ref[pl.ds(start, tk), :])

    _causal_attention(qi, score_fn, va_ref, (s0_ref, s1_ref), m_ref, acc_ref, p_ref, o_ref,
                      tq=tq, tk=tk)


def _mla_attention(q, k, v, *, tq=ATTN_TQ, tk=ATTN_TK):
    H, S, _ = q.shape
    assert S % tq == 0 and tq % tk == 0
    return pl.pallas_call(
        functools.partial(_mla_attn_kernel, tq=tq, tk=tk),
        grid=(H, S // tq),
        in_specs=[
            pl.BlockSpec((None, tq, HEAD_PAD), lambda h, i: (h, i, 0)),
            pl.BlockSpec((None, S, HEAD_PAD), lambda h, i: (h, 0, 0)),
            pl.BlockSpec((None, S, V_DIM), lambda h, i: (h, 0, 0)),
        ],
        out_specs=pl.BlockSpec((tq, V_DIM), lambda h, i: (i, h)),
        out_shape=jax.ShapeDtypeStruct((S, H * V_DIM), BF16),
        scratch_shapes=_attn_scratch(S, tq, tk, V_DIM),
        compiler_params=_params(_ARB2),
        name="mla_attn",
    )(q, k, v)


def _rotate_half_cols(w):
    w1, w2 = jnp.split(w, 2, axis=-1)
    return jnp.concatenate([-w2, w1], axis=-1)


def _alibi_slopes(n):
    return jnp.asarray(np.array([2.0 ** (-8.0 * (i + 1) / n) for i in range(n)], dtype=np.float32))


def _rope_table(s):
    inv = 1.0 / (ROPE_THETA ** (jnp.arange(0, QK_ROPE, 2, dtype=F32) / QK_ROPE))
    ang = jnp.arange(s, dtype=F32)[:, None] * inv[None, :]
    cos, sin = jnp.cos(ang), jnp.sin(ang)
    return jnp.concatenate([cos, cos, sin, sin], axis=-1)


def kernel(x, ln_ffn1, ffn1_wgu, ffn1_wd, ln_mix, ln_ffn2, ffn2_wgu, ffn2_wd, moba_wqkv, moba_wo, kv_norm, mla_wdkv, ckv_norm, mla_wukv, mla_wdq, cq_norm, mla_wuq, mla_wo, final_norm):
    B, S, D = x.shape
    assert B == 1
    depth = ln_ffn1.shape[0]
    n_a = moba_wqkv.shape[0]
    n_b = mla_wdq.shape[0]
    H = N_HEADS

    ln1 = ln_ffn1.reshape(depth, 1, D)
    lnm = ln_mix.reshape(depth, 1, D)
    ln2 = ln_ffn2.reshape(depth, 1, D)
    wqkv, wo_a = moba_wqkv, moba_wo.astype(BF16)
    wo_b = mla_wo.astype(BF16)
    slopes = _alibi_slopes(H)
    cs = _rope_table(S)

    kr = mla_wdkv[:, KV_LORA:]
    wdkv = jnp.concatenate([mla_wdkv, _rotate_half_cols(kr)], axis=-1).astype(BF16)
    wukv = mla_wukv.astype(BF16)
    wdq = mla_wdq.astype(BF16)
    R = mla_wuq.shape[1]
    wuq4 = mla_wuq.reshape(n_b, R, H, QK_NOPE + QK_ROPE)
    rope_cols = wuq4[..., QK_NOPE:]
    wuq = jnp.concatenate([wuq4, _rotate_half_cols(rope_cols)], axis=-1)
    wuq = wuq.reshape(n_b, R, H * HEAD_PAD).astype(BF16)
    cqn = cq_norm.reshape(n_b, 1, R)

    h = x.reshape(S, D)

    fg = final_norm.reshape(1, D)

    def macaron(h, l, mixer):
        h = _ffn(h, ln1, ffn1_wgu, ffn1_wd, l)
        h = mixer(h)
        return _ffn(h, ln2, ffn2_wgu, ffn2_wd, l, final_gain=fg if l == depth - 1 else None)

    for l in range(n_a):
        def moba(h, l=l):
            qkv = _moba_qkv(h, lnm, wqkv, l)
            o = _moba_attention(qkv, slopes)
            return _oproj(h, o, wo_a, l)
        h = macaron(h, l, moba)

    k_sh, v_sh = _mla_kv(h, kv_norm.reshape(1, D), wdkv, ckv_norm.reshape(1, KV_LORA), wukv, cs)

    for j in range(n_b):
        l = n_a + j

        def mla(h, l=l, j=j):
            q = _mla_q(h, lnm, wdq, cqn, wuq, cs, l, j)
            o = _mla_attention(q, k_sh, v_sh)
            return _oproj(h, o, wo_b, j)
        h = macaron(h, l, mla)

    return h.reshape(B, S, D)
```
